```python
import math
import numpy as np
import jax
import jax.numpy as jnp
from jax import lax


D_MODEL = 1024
BATCH = 16
SEQ = 2048
DEPTH = 2

HEAD_DIM = 64
Q_BLOCK = 128
ROPE_THETA = 10000.0
NORM_EPS = 1e-6
NEG_INF = -1e30
MACARON_WEIGHT = 0.5
FFN_DIM = 2816
MEM_LEN = 256
XA_HEADS = 4
XA_HEAD_DIM = D_MODEL // XA_HEADS

MLA_HEADS = 4
MLA_Q_LORA = 192
MLA_KV_LORA = 128
MLA_NOPE_DIM = 64
MLA_ROPE_DIM = 32
MLA_V_DIM = 64

NSA_HEADS = 4
NSA_CMP_LEN = 32
NSA_CMP_STRIDE = 16
NSA_CMP_HIDDEN = 256
NSA_SEL_LEN = 64
NSA_SEL_BLOCKS = 8
NSA_SEL_Q_BLOCK = 64
NSA_WINDOW = 256
NSA_FORCE_SCORE = 1e4

FOX_HEADS = 4

S5_WIDTH = 256
S5_GROUP = 16
S5_GROUPS = S5_WIDTH // S5_GROUP
S5_STATE = 64
S5_DT_MIN = 1e-3
S5_DT_MAX = 1e-1

N_MIXERS = 4
GROUP_WIDTH = 256
MIX_WIDTH = N_MIXERS * GROUP_WIDTH
MLA_IN = MLA_Q_LORA + MLA_KV_LORA + MLA_ROPE_DIM
NSA_IN = NSA_HEADS * HEAD_DIM + 6 * HEAD_DIM + 3 * NSA_HEADS
FOX_IN = 3 * FOX_HEADS * HEAD_DIM + FOX_HEADS
S5_IN = S5_WIDTH
IN_WIDTH = MLA_IN + NSA_IN + FOX_IN + S5_IN

kernel_name = 'hybrid_parallel_mla_nsa_fox_s5_block'


def _rmsnorm(x, g):
    xf = x.astype(jnp.float32)
    y = xf * lax.rsqrt(jnp.mean(xf * xf, axis=-1, keepdims=True) + NORM_EPS)
    return (y * g.astype(jnp.float32)).astype(x.dtype)


def _rope_tables(n, dim):
    inv = 1.0 / (ROPE_THETA ** (jnp.arange(0, dim, 2, dtype=jnp.float32) / dim))
    ang = jnp.arange(n, dtype=jnp.float32)[:, None] * inv[None, :]
    return jnp.cos(ang), jnp.sin(ang)


def _rope(x, cos, sin):
    c = cos[:, None, :].astype(x.dtype)
    s = sin[:, None, :].astype(x.dtype)
    x1, x2 = jnp.split(x, 2, axis=-1)
    return jnp.concatenate([x1 * c - x2 * s, x1 * s + x2 * c], axis=-1)


def _masked_softmax(s, mask):
    s = jnp.where(mask, s, NEG_INF)
    m = jnp.max(s, axis=-1, keepdims=True)
    e = jnp.where(mask, jnp.exp(s - m), 0.0)
    return e / jnp.maximum(jnp.sum(e, axis=-1, keepdims=True), 1e-30)


def _swiglu(x, w_gate, w_up, w_down):
    return (jax.nn.silu(x @ w_gate) * (x @ w_up)) @ w_down


def _causal_block_attention(q, k, v, scale, cum_log_f=None):
    S = q.shape[1]
    pos = jnp.arange(S)
    cum_t = None if cum_log_f is None else jnp.transpose(cum_log_f, (0, 2, 1))
    outs = []
    for i in range(S // Q_BLOCK):
        lo, hi = i * Q_BLOCK, (i + 1) * Q_BLOCK
        s = jnp.einsum('bqhd,bkhd->bhqk', q[:, lo:hi], k[:, :hi]).astype(jnp.float32) * scale
        if cum_t is not None:
            s = s + cum_t[:, :, lo:hi, None] - cum_t[:, :, None, :hi]
        mask = pos[lo:hi, None] >= pos[None, :hi]
        p = _masked_softmax(s, mask)
        outs.append(jnp.einsum('bhqk,bkhd->bqhd', p.astype(v.dtype), v[:, :hi]))
    return jnp.concatenate(outs, axis=1)


def _mla(c_q, c_kv, k_pe, q_norm_g, w_uq, kv_norm_g, w_ukv, rope):
    B, S, _ = c_q.shape
    H = MLA_HEADS
    q = (_rmsnorm(c_q, q_norm_g) @ w_uq).reshape(B, S, H, MLA_NOPE_DIM + MLA_ROPE_DIM)
    q = jnp.concatenate([q[..., :MLA_NOPE_DIM], _rope(q[..., MLA_NOPE_DIM:], *rope)], axis=-1)
    kv = (_rmsnorm(c_kv, kv_norm_g) @ w_ukv).reshape(B, S, H, MLA_NOPE_DIM + MLA_V_DIM)
    k_pe = jnp.broadcast_to(_rope(k_pe[:, :, None, :], *rope), (B, S, H, MLA_ROPE_DIM))
    k = jnp.concatenate([kv[..., :MLA_NOPE_DIM], k_pe], axis=-1)
    o = _causal_block_attention(q, k, kv[..., MLA_NOPE_DIM:], (MLA_NOPE_DIM + MLA_ROPE_DIM) ** -0.5)
    return o.reshape(B, S, H * MLA_V_DIM)


def _nsa(q, k_cmp, v_cmp, k_slc, v_slc, k_swa, v_swa, gate_logit, cmp_pos, phi_w1, phi_b1, phi_w2, gate_b, rope):
    B, S, _ = q.shape
    H, d = NSA_HEADS, HEAD_DIM
    scale = d ** -0.5
    pos = jnp.arange(S)
    q = _rope(q.reshape(B, S, H, d), *rope)

    def rope1(t):
        return _rope(t[:, :, None, :], *rope)[:, :, 0, :]

    k_cmp, k_slc, k_swa = rope1(k_cmp), rope1(k_slc), rope1(k_swa)

    n_cmp = (S - NSA_CMP_LEN) // NSA_CMP_STRIDE + 1
    blk_idx = np.arange(n_cmp)[:, None] * NSA_CMP_STRIDE + np.arange(NSA_CMP_LEN)[None, :]

    def compress(t, j):
        blocks = t[:, blk_idx] + cmp_pos[j]
        hid = jax.nn.gelu(blocks.reshape(B, n_cmp, NSA_CMP_LEN * d) @ phi_w1[j] + phi_b1[j])
        return hid @ phi_w2[j]

    kc, vc = compress(k_cmp, 0), compress(v_cmp, 1)
    cmp_end = jnp.arange(n_cmp) * NSA_CMP_STRIDE + NSA_CMP_LEN - 1
    s_c = jnp.einsum('bshd,bcd->bhsc', q, kc).astype(jnp.float32) * scale
    p_c = _masked_softmax(s_c, pos[:, None] >= cmp_end[None, :])
    o_cmp = jnp.einsum('bhsc,bcd->bshd', p_c.astype(vc.dtype), vc)

    n_blk = S // NSA_SEL_LEN
    n_sel = min(NSA_SEL_BLOCKS, n_blk)
    cs = np.arange(n_cmp) * NSA_CMP_STRIDE
    ss = np.arange(n_blk) * NSA_SEL_LEN
    overlap = ((cs[:, None] < ss[None, :] + NSA_SEL_LEN) & (cs[:, None] + NSA_CMP_LEN > ss[None, :])).astype(np.float32)
    imp = jnp.einsum('bhsc,cj->bsj', p_c, jnp.asarray(overlap))
    blk = jnp.arange(n_blk)[None, :]
    cur = (pos // NSA_SEL_LEN)[:, None]
    forced = (blk == 0) | (blk == cur) | (blk == cur - 1)
    score = jnp.where(forced, NSA_FORCE_SCORE, imp)
    score = jnp.where(blk <= cur, score, NEG_INF)
    _, sel_idx = lax.top_k(score, n_sel)

    kb = k_slc.reshape(B, n_blk, NSA_SEL_LEN, d)
    vb = v_slc.reshape(B, n_blk, NSA_SEL_LEN, d)
    nqb = S // NSA_SEL_Q_BLOCK
    q_blocks = q.reshape(B, nqb, NSA_SEL_Q_BLOCK, H, d).transpose(1, 0, 2, 3, 4)
    idx_blocks = sel_idx.reshape(B, nqb, NSA_SEL_Q_BLOCK, n_sel).transpose(1, 0, 2, 3)
    t_blocks = pos.reshape(nqb, NSA_SEL_Q_BLOCK)
    n_keys = n_sel * NSA_SEL_LEN

    def sel_attend(args):
        qb, ib, tb = args
        ks = jax.vmap(lambda kk, ii: kk[ii])(kb, ib)
        vs = jax.vmap(lambda vv, ii: vv[ii])(vb, ib)
        kpos = ib[..., None] * NSA_SEL_LEN + jnp.arange(NSA_SEL_LEN)
        mask = (kpos <= tb[None, :, None, None]).reshape(B, 1, NSA_SEL_Q_BLOCK, n_keys)
        s = jnp.einsum('bqhd,bqnld->bhqnl', qb, ks).astype(jnp.float32).reshape(B, H, NSA_SEL_Q_BLOCK, n_keys) * scale
        p = _masked_softmax(s, mask)
        return jnp.einsum('bhqm,bqmd->bqhd', p.astype(vs.dtype), vs.reshape(B, NSA_SEL_Q_BLOCK, n_keys, d))

    o_slc = lax.map(sel_attend, (q_blocks, idx_blocks, t_blocks))
    o_slc = o_slc.transpose(1, 0, 2, 3, 4).reshape(B, S, H, d)

    nqw = S // Q_BLOCK
    n_prev = NSA_WINDOW // Q_BLOCK
    pad = ((0, 0), (NSA_WINDOW, 0), (0, 0))
    kp = jnp.pad(k_swa, pad).reshape(B, nqw + n_prev, Q_BLOCK, d)
    vp = jnp.pad(v_swa, pad).reshape(B, nqw + n_prev, Q_BLOCK, d)
    kw = jnp.concatenate([kp[:, o:o + nqw] for o in range(n_prev + 1)], axis=2)
    vw = jnp.concatenate([vp[:, o:o + nqw] for o in range(n_prev + 1)], axis=2)
    qpos = pos.reshape(nqw, Q_BLOCK)
    kpos = (jnp.arange(nqw) * Q_BLOCK - NSA_WINDOW)[:, None] + jnp.arange((n_prev + 1) * Q_BLOCK)[None, :]
    kq = kpos[:, None, :]
    tq = qpos[:, :, None]
    wmask = (kq <= tq) & (kq > tq - NSA_WINDOW) & (kq >= 0)
    s_w = jnp.einsum('bnqhd,bnkd->bnhqk', q.reshape(B, nqw, Q_BLOCK, H, d), kw).astype(jnp.float32) * scale
    p_w = _masked_softmax(s_w, wmask[None, :, None])
    o_swa = jnp.einsum('bnhqk,bnkd->bnqhd', p_w.astype(vw.dtype), vw).reshape(B, S, H, d)

    g = jax.nn.sigmoid((gate_logit + gate_b).astype(jnp.float32)).reshape(B, S, 3, H)
    out = g[:, :, 0, :, None] * o_cmp + g[:, :, 1, :, None] * o_slc + g[:, :, 2, :, None] * o_swa
    return out.reshape(B, S, H * d).astype(q.dtype)


def _fox(q, k, v, f_logit, f_b):
    B, S, _ = q.shape
    shp = (B, S, FOX_HEADS, HEAD_DIM)
    log_f = jax.nn.log_sigmoid((f_logit + f_b).astype(jnp.float32))
    cum = jnp.cumsum(log_f, axis=1)
    o = _causal_block_attention(q.reshape(shp), k.reshape(shp), v.reshape(shp), HEAD_DIM ** -0.5, cum)
    return o.reshape(B, S, FOX_HEADS * HEAD_DIM)


def _s5(u, lam_re, lam_im, log_dt, b_re, b_im, c_re, c_im, d_skip, glu_w, glu_b):
    B, S, _ = u.shape
    f32 = jnp.float32
    uf = u.astype(f32).reshape(B, S, S5_GROUPS, S5_GROUP)
    dt = jnp.exp(log_dt.astype(f32))[:, None]
    lr, li = lam_re.astype(f32), lam_im.astype(f32)
    mag = jnp.exp(lr * dt)
    ar, ai = mag * jnp.cos(li * dt), mag * jnp.sin(li * dt)
    den = lr * lr + li * li
    kr = ((ar - 1.0) * lr + ai * li) / den
    ki = (ai * lr - (ar - 1.0) * li) / den
    bu_r = jnp.einsum('bsgc,gpc->bsgp', uf, b_re.astype(f32))
    bu_i = jnp.einsum('bsgc,gpc->bsgp', uf, b_im.astype(f32))
    br = kr * bu_r - ki * bu_i
    bi = kr * bu_i + ki * bu_r
    a_r = jnp.broadcast_to(ar, (1, S, S5_GROUPS, S5_STATE))
    a_i = jnp.broadcast_to(ai, (1, S, S5_GROUPS, S5_STATE))

    def combine(e1, e2):
        a1r, a1i, b1r, b1i = e1
        a2r, a2i, b2r, b2i = e2
        return (a2r * a1r - a2i * a1i, a2r * a1i + a2i * a1r,
                a2r * b1r - a2i * b1i + b2r, a2r * b1i + a2i * b1r + b2i)

    _, _, xr, xi = lax.associative_scan(combine, (a_r, a_i, br, bi), axis=1)
    y = (jnp.einsum('bsgp,gcp->bsgc', xr, c_re.astype(f32))
         - jnp.einsum('bsgp,gcp->bsgc', xi, c_im.astype(f32))
         + d_skip.astype(f32) * uf)
    y = jax.nn.gelu(y.reshape(B, S, S5_WIDTH))
    out = y * jax.nn.sigmoid(y @ glu_w.astype(f32) + glu_b.astype(f32))
    return out.astype(u.dtype)


def _token_mixing(u, w_in, mla_q_norm_g, mla_w_uq, mla_kv_norm_g, mla_w_ukv,
                  nsa_cmp_pos, nsa_phi_w1, nsa_phi_b1, nsa_phi_w2, nsa_gate_b, fox_f_b,
                  s5_lambda_re, s5_lambda_im, s5_log_dt, s5_b_re, s5_b_im, s5_c_re, s5_c_im,
                  s5_d, s5_glu_w, s5_glu_b, group_g, w_out, rope_mla, rope_nsa):
    B, S, _ = u.shape
    proj = u @ w_in
    p_mla, p_nsa, p_fox, p_s5 = jnp.split(proj, [MLA_IN, MLA_IN + NSA_IN, MLA_IN + NSA_IN + FOX_IN], axis=-1)
    c_q, c_kv, k_pe = jnp.split(p_mla, [MLA_Q_LORA, MLA_Q_LORA + MLA_KV_LORA], axis=-1)
    o_mla = _mla(c_q, c_kv, k_pe, mla_q_norm_g, mla_w_uq, mla_kv_norm_g, mla_w_ukv, rope_mla)
    nq = NSA_HEADS * HEAD_DIM
    q_n, kc, vc, ksl, vsl, kw, vw, g_n = jnp.split(p_nsa, [nq + i * HEAD_DIM for i in range(7)], axis=-1)
    o_nsa = _nsa(q_n, kc, vc, ksl, vsl, kw, vw, g_n, nsa_cmp_pos, nsa_phi_w1, nsa_phi_b1, nsa_phi_w2, nsa_gate_b, rope_nsa)
    fw = FOX_HEADS * HEAD_DIM
    q_f, k_f, v_f, f_f = jnp.split(p_fox, [fw, 2 * fw, 3 * fw], axis=-1)
    o_fox = _fox(q_f, k_f, v_f, f_f, fox_f_b)
    o_s5 = _s5(p_s5, s5_lambda_re, s5_lambda_im, s5_log_dt, s5_b_re, s5_b_im, s5_c_re, s5_c_im, s5_d, s5_glu_w, s5_glu_b)
    y = jnp.stack([o_mla, o_nsa, o_fox, o_s5], axis=2)
    y = _rmsnorm(y, group_g).reshape(B, S, MIX_WIDTH)
    return y @ w_out


def _cross_attention(h, mem_n, w_q, w_kv, w_o):
    B, S, _ = h.shape
    M = mem_n.shape[1]
    q = (h @ w_q).reshape(B, S, XA_HEADS, XA_HEAD_DIM)
    kv = (mem_n @ w_kv).reshape(B, M, 2, XA_HEADS, XA_HEAD_DIM)
    s = jnp.einsum('bshd,bmhd->bhsm', q, kv[:, :, 0]).astype(jnp.float32) * XA_HEAD_DIM ** -0.5
    p = jax.nn.softmax(s, axis=-1)
    o = jnp.einsum('bhsm,bmhd->bshd', p.astype(kv.dtype), kv[:, :, 1]).reshape(B, S, XA_HEADS * XA_HEAD_DIM)
    return o @ w_o


def setup_inputs(seed: int = 0) -> dict:
    key = jax.random.key(seed)
    ks = iter(jax.random.split(key, 64))
    f32 = jnp.float32
    L, G, P = DEPTH, S5_GROUPS, S5_STATE

    def nrm(shape, scale):
        return jax.random.normal(next(ks), shape, f32) * scale

    def gain(shape):
        return 1.0 + 0.02 * jax.random.normal(next(ks), shape, f32)

    n_idx = jnp.arange(P, dtype=f32)
    return {
        'x': nrm((BATCH, SEQ, D_MODEL), 1.0),
        'mem': nrm((BATCH, MEM_LEN, D_MODEL), 1.0),
        'ffn1_pre_g': gain((L, D_MODEL)),
        'ffn1_w_gate': nrm((L, D_MODEL, FFN_DIM), D_MODEL ** -0.5),
        'ffn1_w_up': nrm((L, D_MODEL, FFN_DIM), D_MODEL ** -0.5),
        'ffn1_w_down': nrm((L, FFN_DIM, D_MODEL), FFN_DIM ** -0.5),
        'ffn1_post_g': gain((L, D_MODEL)),
        'mix_pre_g': gain((L, D_MODEL)),
        'mix_w_in': nrm((L, D_MODEL, IN_WIDTH), D_MODEL ** -0.5),
        'mla_q_norm_g': gain((L, MLA_Q_LORA)),
        'mla_w_uq': nrm((L, MLA_Q_LORA, MLA_HEADS * (MLA_NOPE_DIM + MLA_ROPE_DIM)), MLA_Q_LORA ** -0.5),
        'mla_kv_norm_g': gain((L, MLA_KV_LORA)),
        'mla_w_ukv': nrm((L, MLA_KV_LORA, MLA_HEADS * (MLA_NOPE_DIM + MLA_V_DIM)), MLA_KV_LORA ** -0.5),
        'nsa_cmp_pos': nrm((L, 2, NSA_CMP_LEN, HEAD_DIM), 0.1),
        'nsa_phi_w1': nrm((L, 2, NSA_CMP_LEN * HEAD_DIM, NSA_CMP_HIDDEN), (NSA_CMP_LEN * HEAD_DIM) ** -0.5),
        'nsa_phi_b1': nrm((L, 2, NSA_CMP_HIDDEN), 0.02),
        'nsa_phi_w2': nrm((L, 2, NSA_CMP_HIDDEN, HEAD_DIM), NSA_CMP_HIDDEN ** -0.5),
        'nsa_gate_b': nrm((L, 3 * NSA_HEADS), 0.02),
        'fox_f_b': 2.0 + nrm((L, FOX_HEADS), 0.5),
        's5_lambda_re': -0.5 * (1.0 + nrm((L, G, P), 0.01)),
        's5_lambda_im': jnp.pi * n_idx * (1.0 + nrm((L, G, P), 0.01)),
        's5_log_dt': jax.random.uniform(next(ks), (L, G), f32, math.log(S5_DT_MIN), math.log(S5_DT_MAX)),
        's5_b_re': nrm((L, G, P, S5_GROUP), S5_GROUP ** -0.5),
        's5_b_im': nrm((L, G, P, S5_GROUP), S5_GROUP ** -0.5),
        's5_c_re': nrm((L, G, S5_GROUP, P), P ** -0.5),
        's5_c_im': nrm((L, G, S5_GROUP, P), P ** -0.5),
        's5_d': nrm((L, G, S5_GROUP), 0.1),
        's5_glu_w': nrm((L, S5_WIDTH, S5_WIDTH), S5_WIDTH ** -0.5),
        's5_glu_b': nrm((L, S5_WIDTH), 0.02),
        'mix_group_g': gain((L, N_MIXERS, GROUP_WIDTH)),
        'mix_w_out': nrm((L, MIX_WIDTH, D_MODEL), MIX_WIDTH ** -0.5),
        'mix_post_g': gain((L, D_MODEL)),
        'xa_pre_g': gain((L, D_MODEL)),
        'xa_mem_g': gain((L, D_MODEL)),
        'xa_w_q': nrm((L, D_MODEL, XA_HEADS * XA_HEAD_DIM), D_MODEL ** -0.5),
        'xa_w_kv': nrm((L, D_MODEL, 2 * XA_HEADS * XA_HEAD_DIM), D_MODEL ** -0.5),
        'xa_w_o': nrm((L, XA_HEADS * XA_HEAD_DIM, D_MODEL), (XA_HEADS * XA_HEAD_DIM) ** -0.5),
        'xa_post_g': gain((L, D_MODEL)),
        'ffn2_pre_g': gain((L, D_MODEL)),
        'ffn2_w_gate': nrm((L, D_MODEL, FFN_DIM), D_MODEL ** -0.5),
        'ffn2_w_up': nrm((L, D_MODEL, FFN_DIM), D_MODEL ** -0.5),
        'ffn2_w_down': nrm((L, FFN_DIM, D_MODEL), FFN_DIM ** -0.5),
        'ffn2_post_g': gain((L, D_MODEL)),
    }


def reference(x, mem, ffn1_pre_g, ffn1_w_gate, ffn1_w_up, ffn1_w_down, ffn1_post_g,
              mix_pre_g, mix_w_in, mla_q_norm_g, mla_w_uq, mla_kv_norm_g, mla_w_ukv,
              nsa_cmp_pos, nsa_phi_w1, nsa_phi_b1, nsa_phi_w2, nsa_gate_b, fox_f_b,
              s5_lambda_re, s5_lambda_im, s5_log_dt, s5_b_re, s5_b_im, s5_c_re, s5_c_im,
              s5_d, s5_glu_w, s5_glu_b, mix_group_g, mix_w_out, mix_post_g,
              xa_pre_g, xa_mem_g, xa_w_q, xa_w_kv, xa_w_o, xa_post_g,
              ffn2_pre_g, ffn2_w_gate, ffn2_w_up, ffn2_w_down, ffn2_post_g):
    S = x.shape[1]
    rope_mla = _rope_tables(S, MLA_ROPE_DIM)
    rope_nsa = _rope_tables(S, HEAD_DIM)
    h = x
    for l in range(DEPTH):
        u = _rmsnorm(h, ffn1_pre_g[l])
        h = h + MACARON_WEIGHT * _rmsnorm(_swiglu(u, ffn1_w_gate[l], ffn1_w_up[l], ffn1_w_down[l]), ffn1_post_g[l])

        u = _rmsnorm(h, mix_pre_g[l])
        y = _token_mixing(u, mix_w_in[l], mla_q_norm_g[l], mla_w_uq[l], mla_kv_norm_g[l], mla_w_ukv[l],
                          nsa_cmp_pos[l], nsa_phi_w1[l], nsa_phi_b1[l], nsa_phi_w2[l], nsa_gate_b[l], fox_f_b[l],
                          s5_lambda_re[l], s5_lambda_im[l], s5_log_dt[l], s5_b_re[l], s5_b_im[l],
                          s5_c_re[l], s5_c_im[l], s5_d[l], s5_glu_w[l], s5_glu_b[l],
                          mix_group_g[l], mix_w_out[l], rope_mla, rope_nsa)
        h = h + _rmsnorm(y, mix_post_g[l])

        u = _rmsnorm(h, xa_pre_g[l])
        m = _rmsnorm(mem, xa_mem_g[l])
        h = h + _rmsnorm(_cross_attention(u, m, xa_w_q[l], xa_w_kv[l], xa_w_o[l]), xa_post_g[l])

        u = _rmsnorm(h, ffn2_pre_g[l])
        h = h + MACARON_WEIGHT * _rmsnorm(_swiglu(u, ffn2_w_gate[l], ffn2_w_up[l], ffn2_w_down[l]), ffn2_post_g[l])
    return h
```

```python
import functools
import math

import numpy as np
import jax
import jax.numpy as jnp
from jax import lax
from jax.experimental import pallas as pl
from jax.experimental.pallas import tpu as pltpu

F32 = jnp.float32
BF16 = jnp.bfloat16

D_MODEL = 1024
HEAD_DIM = 64
ROPE_THETA = 10000.0
NORM_EPS = 1e-6
NEG_INF = -1e30
MACARON_WEIGHT = 0.5
FFN_DIM = 2816
XA_HEADS = 4
XA_HEAD_DIM = D_MODEL // XA_HEADS

MLA_HEADS = 4
MLA_Q_LORA = 192
MLA_KV_LORA = 128
MLA_NOPE_DIM = 64
MLA_ROPE_DIM = 32
MLA_V_DIM = 64
MLA_PAD_DIM = 128

NSA_HEADS = 4
NSA_CMP_LEN = 32
NSA_CMP_STRIDE = 16
NSA_CMP_HIDDEN = 256
NSA_SEL_LEN = 64
NSA_SEL_BLOCKS = 8
NSA_WINDOW = 256
NSA_FORCE_SCORE = 1e4

FOX_HEADS = 4

S5_WIDTH = 256
S5_GROUP = 16
S5_GROUPS = S5_WIDTH // S5_GROUP
S5_STATE = 64
S5_STATES = S5_GROUPS * S5_STATE

GROUP_WIDTH = 256
LANES = 128
VMEM_LIMIT = 56 * 1024 * 1024

SEG_CQ, SEG_CKV, SEG_KPE, SEG_NQ = 0, 256, 384, 512
SEG_NCMP, SEG_NSLC, SEG_NSWA, SEG_MISC = 768, 896, 1024, 1152
SEG_FQ, SEG_FK, SEG_FV, SEG_S5, IN_PACKED = 1280, 1536, 1792, 2048, 2304
N_GATES = 3 * NSA_HEADS


def _params(sem):
    return pltpu.CompilerParams(dimension_semantics=sem, vmem_limit_bytes=VMEM_LIMIT)


def _rms(x, g, n=None):
    n = x.shape[-1] if n is None else n
    ms = jnp.sum(x * x, axis=-1, keepdims=True) * (1.0 / n)
    return x * lax.rsqrt(ms + NORM_EPS) * g


def _dot(a, b):
    return jnp.dot(a, b, preferred_element_type=F32)


def _dot_nt(a, b):
    return lax.dot_general(a, b, (((1,), (1,)), ((), ())), preferred_element_type=F32)


def _split3(x):
    hi = x.astype(BF16)
    r = x - hi.astype(F32)
    mid = r.astype(BF16)
    lo = (r - mid.astype(F32)).astype(BF16)
    return hi, mid, lo


def _masked_softmax(s, mask):
    s = jnp.where(mask, s, NEG_INF)
    m = jnp.max(s, axis=-1, keepdims=True)
    e = jnp.where(mask, jnp.exp(s - m), 0.0)
    return e / jnp.maximum(jnp.sum(e, axis=-1, keepdims=True), 1e-30)


def _ffn_kernel(h_ref, pre_g_ref, wg_ref, wu_ref, wd_ref, post_g_ref, o_ref, *, chunk):
    h = h_ref[...]
    u = _rms(h, pre_g_ref[...]).astype(BF16)
    acc = jnp.zeros(h.shape, F32)
    for c in range(wg_ref.shape[1] // chunk):
        sl = slice(c * chunk, (c + 1) * chunk)
        g = _dot(u, wg_ref[:, sl])
        p = _dot(u, wu_ref[:, sl])
        a = (g * jax.nn.sigmoid(g)) * p
        acc = acc + _dot(a.astype(BF16), wd_ref[sl, :])
    o_ref[...] = h + MACARON_WEIGHT * _rms(acc, post_g_ref[...])


def _ffn(h, pre_g, wg, wu, wd, post_g, layer, tm=512, chunk=256):
    T, D = h.shape
    F = wg.shape[-1]
    wspec = functools.partial(pl.BlockSpec, pipeline_mode=pl.Buffered(1))
    return pl.pallas_call(
        functools.partial(_ffn_kernel, chunk=chunk),
        grid=(T // tm,),
        in_specs=[
            pl.BlockSpec((tm, D), lambda i: (i, 0)),
            pl.BlockSpec((None, 1, D), lambda i: (layer, 0, 0)),
            wspec((None, D, F), lambda i: (layer, 0, 0)),
            wspec((None, D, F), lambda i: (layer, 0, 0)),
            wspec((None, F, D), lambda i: (layer, 0, 0)),
            pl.BlockSpec((None, 1, D), lambda i: (layer, 0, 0)),
        ],
        out_specs=pl.BlockSpec((tm, D), lambda i: (i, 0)),
        out_shape=jax.ShapeDtypeStruct((T, D), F32),
        compiler_params=_params(("parallel",)),
        name="ffn",
    )(h, pre_g, wg, wu, wd, post_g)


def _proj_kernel(h_ref, g_ref, win_ref, qg_ref, wuq_ref, kvg_ref, wukv_ref, bias_ref,
                 cos64_ref, sin64_ref, cosm_ref, sinm_ref,
                 mq_ref, mk_ref, mv_ref, nq_ref, nkc_ref, nvc_ref, nksl_ref, nvsl_ref, nkw_ref, nvw_ref,
                 misc_ref, fq_ref, fk_ref, fv_ref, s5_ref):
    h = h_ref[0]
    tm = h.shape[0]
    u = _rms(h, g_ref[...]).astype(BF16)
    z = _dot(u, win_ref[...])
    lane = lax.broadcasted_iota(jnp.int32, (tm, LANES), 1)

    def rope(x, c, s, half):
        fwd = pltpu.roll(x, LANES - half, 1)
        bwd = pltpu.roll(x, half, 1)
        return x * c + jnp.where(lane % (2 * half) < half, fwd, bwd) * s

    cos64, sin64 = cos64_ref[...], sin64_ref[...]
    cosm, sinm = cosm_ref[...], sinm_ref[...]
    half_m = MLA_ROPE_DIM // 2
    half_n = HEAD_DIM // 2

    cq = _rms(z[:, SEG_CQ:SEG_CKV], qg_ref[...], MLA_Q_LORA).astype(BF16)
    q = _dot(cq, wuq_ref[...])
    ckv = _rms(z[:, SEG_CKV:SEG_KPE], kvg_ref[...]).astype(BF16)
    kv = _dot(ckv, wukv_ref[...])
    kpe = rope(z[:, SEG_KPE:SEG_NQ], cosm, sinm, half_m)
    mla_scale = (MLA_NOPE_DIM + MLA_ROPE_DIM) ** -0.5
    for hh in range(MLA_HEADS):
        sl = slice(hh * MLA_PAD_DIM, (hh + 1) * MLA_PAD_DIM)
        mq_ref[0, hh] = (rope(q[:, sl], cosm, sinm, half_m) * mla_scale).astype(BF16)
        mk_ref[0, hh] = (kv[:, sl] + kpe).astype(BF16)
        v0 = MLA_HEADS * MLA_PAD_DIM + hh * MLA_V_DIM
        mv_ref[0, hh] = kv[:, v0:v0 + MLA_V_DIM].astype(BF16)

    scale = HEAD_DIM ** -0.5
    for c in range(NSA_HEADS // 2):
        r = rope(z[:, SEG_NQ + c * LANES:SEG_NQ + (c + 1) * LANES], cos64, sin64, half_n) * scale
        nq_ref[0, 2 * c] = r[:, :HEAD_DIM].astype(BF16)
        nq_ref[0, 2 * c + 1] = r[:, HEAD_DIM:].astype(BF16)
    seg = z[:, SEG_NCMP:SEG_NSLC]
    nkc_ref[0] = rope(seg, cos64, sin64, half_n)[:, :HEAD_DIM]
    nvc_ref[0] = seg[:, HEAD_DIM:]
    seg = z[:, SEG_NSLC:SEG_NSWA]
    nksl_ref[0] = rope(seg, cos64, sin64, half_n)[:, :HEAD_DIM].astype(BF16)
    nvsl_ref[0] = seg[:, HEAD_DIM:].astype(BF16)
    seg = z[:, SEG_NSWA:SEG_MISC]
    nkw_ref[0] = rope(seg, cos64, sin64, half_n)[:, :HEAD_DIM].astype(BF16)
    nvw_ref[0] = seg[:, HEAD_DIM:].astype(BF16)

    x = z[:, SEG_MISC:SEG_FQ] + bias_ref[...]
    log_sig = jnp.minimum(x, 0.0) - jnp.log(1.0 + jnp.exp(-jnp.abs(x)))
    misc_ref[0] = jnp.where(lane < N_GATES, jax.nn.sigmoid(x), log_sig)

    for hh in range(FOX_HEADS):
        sl = slice(hh * HEAD_DIM, (hh + 1) * HEAD_DIM)
        fq_ref[0, hh] = (z[:, SEG_FQ:SEG_FK][:, sl] * scale).astype(BF16)
        fk_ref[0, hh] = z[:, SEG_FK:SEG_FV][:, sl].astype(BF16)
        fv_ref[0, hh] = z[:, SEG_FV:SEG_S5][:, sl].astype(BF16)

    s5_ref[0] = z[:, SEG_S5:IN_PACKED]


def _proj(h, g, win, qg, wuq, kvg, wukv, bias, cos64, sin64, cosm, sinm, layer, tm=512):
    B, S, D = h.shape
    nt = S // tm
    wspec = functools.partial(pl.BlockSpec, pipeline_mode=pl.Buffered(1))

    def w2(shape):
        return wspec((None,) + shape, lambda b, t: (layer, 0, 0))

    def vec(n):
        return pl.BlockSpec((None, 1, n), lambda b, t: (layer, 0, 0))

    def tab():
        return pl.BlockSpec((tm, LANES), lambda b, t: (t, 0))

    def heads(d):
        return pl.BlockSpec((1, 4, tm, d), lambda b, t: (b, 0, t, 0))

    def tok(d):
        return pl.BlockSpec((1, tm, d), lambda b, t: (b, t, 0))

    def hshape(d, dt=BF16):
        return jax.ShapeDtypeStruct((B, 4, S, d), dt)

    def tshape(d, dt):
        return jax.ShapeDtypeStruct((B, S, d), dt)

    return pl.pallas_call(
        _proj_kernel,
        grid=(B, nt),
        in_specs=[
            tok(D), vec(D), w2((D, IN_PACKED)), vec(256), w2((256, 4 * MLA_PAD_DIM)),
            vec(MLA_KV_LORA), w2((MLA_KV_LORA, 4 * MLA_PAD_DIM + 4 * MLA_V_DIM)), vec(LANES),
            tab(), tab(), tab(), tab(),
        ],
        out_specs=[
            heads(MLA_PAD_DIM), heads(MLA_PAD_DIM), heads(MLA_V_DIM),
            heads(HEAD_DIM), tok(HEAD_DIM), tok(HEAD_DIM), tok(HEAD_DIM), tok(HEAD_DIM), tok(HEAD_DIM), tok(HEAD_DIM),
            tok(LANES), heads(HEAD_DIM), heads(HEAD_DIM), heads(HEAD_DIM), tok(S5_WIDTH),
        ],
        out_shape=[
            hshape(MLA_PAD_DIM), hshape(MLA_PAD_DIM), hshape(MLA_V_DIM),
            hshape(HEAD_DIM), tshape(HEAD_DIM, F32), tshape(HEAD_DIM, F32),
            tshape(HEAD_DIM, BF16), tshape(HEAD_DIM, BF16), tshape(HEAD_DIM, BF16), tshape(HEAD_DIM, BF16),
            tshape(LANES, F32), hshape(HEAD_DIM), hshape(HEAD_DIM), hshape(HEAD_DIM), tshape(S5_WIDTH, F32),
        ],
        compiler_params=_params(("parallel", "parallel")),
        name="mix_proj",
    )(h, g, win, qg, wuq, kvg, wukv, bias, cos64, sin64, cosm, sinm)


def _cumsum_kernel(x_ref, o_ref, *, blk):
    S = x_ref.shape[1]
    r = lax.broadcasted_iota(jnp.int32, (blk, blk), 0)
    c = lax.broadcasted_iota(jnp.int32, (blk, blk), 1)
    tri = jnp.where(r >= c, 1.0, 0.0).astype(BF16)
    carry = jnp.zeros((1, x_ref.shape[2]), F32)
    for i in range(S // blk):
        hi, mid, lo = _split3(x_ref[0, i * blk:(i + 1) * blk, :])
        cs = _dot(tri, hi) + _dot(tri, mid) + _dot(tri, lo) + carry
        o_ref[0, i * blk:(i + 1) * blk, :] = cs
        carry = cs[blk - 1:blk, :]


def _cumsum(x, blk=256):
    B, S, W = x.shape
    return pl.pallas_call(
        functools.partial(_cumsum_kernel, blk=blk),
        grid=(B,),
        in_specs=[pl.BlockSpec((1, S, W), lambda b: (b, 0, 0))],
        out_specs=pl.BlockSpec((1, S, W), lambda b: (b, 0, 0)),
        out_shape=jax.ShapeDtypeStruct((B, S, W), F32),
        compiler_params=_params(("parallel",)),
        name="fox_cumsum",
    )(x)


def _causal_attn_kernel(*refs, tq, has_bias):
    if has_bias:
        q_ref, k_ref, v_ref, cq_ref, ck_ref, o_ref = refs
    else:
        q_ref, k_ref, v_ref, o_ref = refs
    qi = pl.program_id(2)
    q = q_ref[0, 0]
    dv = v_ref.shape[-1]
    cq = cq_ref[0, 0] if has_bias else None

    def tile(j, carry, diag):
        m, l, acc = carry
        off = pl.multiple_of(j * tq, tq)
        k = k_ref[0, 0, pl.ds(off, tq), :]
        v = v_ref[0, 0, pl.ds(off, tq), :]
        s = _dot_nt(q, k)
        if has_bias:
            s = s + cq - ck_ref[0, 0, :, pl.ds(off, tq)]
        if diag:
            r = lax.broadcasted_iota(jnp.int32, (tq, tq), 0)
            c = lax.broadcasted_iota(jnp.int32, (tq, tq), 1)
            mask = r >= c
            s = jnp.where(mask, s, NEG_INF)
        m_new = jnp.maximum(m, jnp.max(s, axis=-1, keepdims=True))
        p = jnp.exp(s - m_new)
        if diag:
            p = jnp.where(mask, p, 0.0)
        alpha = jnp.exp(m - m_new)
        l = alpha * l + jnp.sum(p, axis=-1, keepdims=True)
        acc = alpha * acc + _dot(p.astype(BF16), v)
        return m_new, l, acc

    init = (jnp.full((tq, 1), NEG_INF, F32), jnp.zeros((tq, 1), F32), jnp.zeros((tq, dv), F32))
    carry = lax.fori_loop(0, qi, lambda j, c: tile(j, c, False), init)
    m, l, acc = tile(qi, carry, True)
    o_ref[0, 0] = acc / jnp.maximum(l, 1e-30)


def _causal_attn(q, k, v, cum_col=None, cum_row=None, tq=256):
    B, H, S, dk = q.shape
    dv = v.shape[-1]
    has_bias = cum_col is not None
    in_specs = [
        pl.BlockSpec((1, 1, tq, dk), lambda b, h, i: (b, h, i, 0)),
        pl.BlockSpec((1, 1, S, dk), lambda b, h, i: (b, h, 0, 0)),
        pl.BlockSpec((1, 1, S, dv), lambda b, h, i: (b, h, 0, 0)),
    ]
    args = [q, k, v]
    if has_bias:
        in_specs += [
            pl.BlockSpec((1, 1, tq, 1), lambda b, h, i: (b, h, i, 0)),
            pl.BlockSpec((1, 1, 1, S), lambda b, h, i: (b, h, 0, 0)),
        ]
        args += [cum_col, cum_row]
    return pl.pallas_call(
        functools.partial(_causal_attn_kernel, tq=tq, has_bias=has_bias),
        grid=(B, H, S // tq),
        in_specs=in_specs,
        out_specs=pl.BlockSpec((1, 1, tq, dv), lambda b, h, i: (b, h, i, 0)),
        out_shape=jax.ShapeDtypeStruct((B, H, S, dv), F32),
        compiler_params=_params(("parallel", "parallel", "arbitrary")),
        name="fox_attn" if has_bias else "mla_attn",
    )(*args)


def _nsa_compress_kernel(t_ref, pos_ref, w1_ref, b1_ref, w2_ref, o_ref):
    t = t_ref[0, 0]
    half = t.shape[1]
    a = _dot((t + pos_ref[0, :, :half]).astype(BF16), w1_ref[0, :half, :])
    b = _dot((t + pos_ref[0, :, half:]).astype(BF16), w1_ref[0, half:, :])
    nrow = t.shape[0]
    hid = jax.nn.gelu(a + pltpu.roll(b, nrow - 1, 0) + b1_ref[0])
    o_ref[0, 0] = _dot(hid.astype(BF16), w2_ref[0]).astype(BF16)


def _nsa_compress(t2, pos, w1, b1, w2, layer):
    _, B, R, W = t2.shape
    return pl.pallas_call(
        _nsa_compress_kernel,
        grid=(2, B),
        in_specs=[
            pl.BlockSpec((1, 1, R, W), lambda j, b: (j, b, 0, 0)),
            pl.BlockSpec((1, 1, 2 * W), lambda j, b: (2 * layer + j, 0, 0)),
            pl.BlockSpec((1, 2 * W, NSA_CMP_HIDDEN), lambda j, b: (2 * layer + j, 0, 0)),
            pl.BlockSpec((1, 1, NSA_CMP_HIDDEN), lambda j, b: (2 * layer + j, 0, 0)),
            pl.BlockSpec((1, NSA_CMP_HIDDEN, HEAD_DIM), lambda j, b: (2 * layer + j, 0, 0)),
        ],
        out_specs=pl.BlockSpec((1, 1, R, HEAD_DIM), lambda j, b: (j, b, 0, 0)),
        out_shape=jax.ShapeDtypeStruct((2, B, R, HEAD_DIM), BF16),
        compiler_params=_params(("parallel", "parallel")),
        name="nsa_compress",
    )(t2, pos, w1, b1, w2)


def _nsa_attn_kernel(q_ref, cmp_ref, ksl_ref, vsl_ref, kw_ref, vw_ref, gate_ref, ovl_ref, exp_ref, o_ref, *, tq):
    H = NSA_HEADS
    qi = pl.program_id(1)
    lo = qi * tq
    q4 = q_ref[0].reshape(H * tq, HEAD_DIM)
    pos_q = lo + lax.broadcasted_iota(jnp.int32, (tq, 1), 0)
    lane = lax.broadcasted_iota(jnp.int32, (1, LANES), 1)

    kc, vc = cmp_ref[0, 0], cmp_ref[1, 0]
    nc = kc.shape[0]
    cmp_end = lax.broadcasted_iota(jnp.int32, (1, nc), 1) * NSA_CMP_STRIDE + (NSA_CMP_LEN - 1)
    mask_c = (pos_q >= cmp_end)[None]
    p_c = _masked_softmax(_dot_nt(q4, kc).reshape(H, tq, nc), mask_c)
    o_cmp = _dot(p_c.reshape(H * tq, nc).astype(BF16), vc).reshape(H, tq, HEAD_DIM)

    p_sum = p_c[0] + p_c[1] + p_c[2] + p_c[3]
    hi, mid, lo3 = _split3(p_sum)
    ovl = ovl_ref[...]
    imp = _dot(hi, ovl) + _dot(mid, ovl) + _dot(lo3, ovl)
    cur = pos_q // NSA_SEL_LEN
    forced = (lane == 0) | (lane == cur) | (lane == cur - 1)
    score = jnp.where(forced, NSA_FORCE_SCORE, imp)
    score = jnp.where(lane <= cur, score, NEG_INF)
    n_blk = exp_ref.shape[0] * exp_ref.shape[2] // NSA_SEL_LEN
    cnt = jnp.zeros(score.shape, F32)
    for kk in range(n_blk):
        col = score[:, kk:kk + 1]
        beats = (col > score) | ((col == score) & (lane > kk))
        cnt = cnt + jnp.where(beats, 1.0, 0.0)
    sel = jnp.where(cnt < min(NSA_SEL_BLOCKS, n_blk), 1.0, 0.0).astype(BF16)

    def tile(j, carry):
        m, l, acc = carry
        off = pl.multiple_of(j * tq, tq)
        k = ksl_ref[0, pl.ds(off, tq), :]
        v = vsl_ref[0, pl.ds(off, tq), :]
        s = _dot_nt(q4, k).reshape(H, tq, tq)
        sel_k = _dot(sel, exp_ref[j])
        kpos = off + lax.broadcasted_iota(jnp.int32, (1, tq), 1)
        mask = ((sel_k > 0.5) & (kpos <= pos_q))[None]
        s = jnp.where(mask, s, NEG_INF)
        m_new = jnp.maximum(m, jnp.max(s, axis=-1, keepdims=True))
        p = jnp.where(mask, jnp.exp(s - m_new), 0.0)
        alpha = jnp.exp(m - m_new)
        l = alpha * l + jnp.sum(p, axis=-1, keepdims=True)
        pv = _dot(p.reshape(H * tq, tq).astype(BF16), v).reshape(H, tq, HEAD_DIM)
        return m_new, l, alpha * acc + pv

    init = (jnp.full((H, tq, 1), NEG_INF, F32), jnp.zeros((H, tq, 1), F32), jnp.zeros((H, tq, HEAD_DIM), F32))
    _, l, acc = lax.fori_loop(0, qi + 1, tile, init)
    o_slc = acc / jnp.maximum(l, 1e-30)

    nw = NSA_WINDOW + tq
    start = pl.multiple_of(jnp.maximum(lo - NSA_WINDOW, 0), tq)
    kw = kw_ref[0, pl.ds(start, nw), :]
    vw = vw_ref[0, pl.ds(start, nw), :]
    kpos = start + lax.broadcasted_iota(jnp.int32, (1, nw), 1)
    mask_w = ((kpos <= pos_q) & (kpos > pos_q - NSA_WINDOW))[None]
    p_w = _masked_softmax(_dot_nt(q4, kw).reshape(H, tq, nw), mask_w)
    o_swa = _dot(p_w.reshape(H * tq, nw).astype(BF16), vw).reshape(H, tq, HEAD_DIM)

    g = gate_ref[0]
    for hh in range(H):
        o_ref[0, hh] = (g[:, hh:hh + 1] * o_cmp[hh] + g[:, H + hh:H + hh + 1] * o_slc[hh]
                        + g[:, 2 * H + hh:2 * H + hh + 1] * o_swa[hh])


def _nsa_attn(q, cmp, ksl, vsl, kw, vw, gates, ovl, expand, tq=256):
    B, H, S, d = q.shape
    R = cmp.shape[2]

    def full(w):
        return pl.BlockSpec((1, S, w), lambda b, i: (b, 0, 0))

    return pl.pallas_call(
        functools.partial(_nsa_attn_kernel, tq=tq),
        grid=(B, S // tq),
        in_specs=[
            pl.BlockSpec((1, H, tq, d), lambda b, i: (b, 0, i, 0)),
            pl.BlockSpec((2, 1, R, d), lambda b, i: (0, b, 0, 0)),
            full(d), full(d), full(d), full(d),
            pl.BlockSpec((1, tq, LANES), lambda b, i: (b, i, 0)),
            pl.BlockSpec((R, LANES), lambda b, i: (0, 0)),
            pl.BlockSpec((S // tq, LANES, tq), lambda b, i: (0, 0, 0)),
        ],
        out_specs=pl.BlockSpec((1, H, tq, d), lambda b, i: (b, 0, i, 0)),
        out_shape=jax.ShapeDtypeStruct((B, H, S, d), F32),
        compiler_params=_params(("parallel", "arbitrary")),
        name="nsa_attn",
    )(q, cmp, ksl, vsl, kw, vw, gates, ovl, expand)


def _s5_kernel(u_ref, bre_ref, bim_ref, kr_ref, ki_ref, ar_ref, ai_ref, cre_ref, cim_ref, d_ref, y_ref,
               xr_ref, xi_ref, sr_ref, si_ref, *, tc, nb, lane_blk):
    @pl.when(pl.program_id(0) == 0)
    def _():
        sr_ref[...] = jnp.zeros(sr_ref.shape, F32)
        si_ref[...] = jnp.zeros(si_ref.shape, F32)

    u = u_ref[...].reshape(tc * nb, S5_WIDTH)
    ub = u.astype(BF16)
    kr, ki = kr_ref[...], ki_ref[...]
    bre, bim = bre_ref[...], bim_ref[...]
    xr_ref[...] = _dot(ub, (bre * kr - bim * ki).astype(BF16))
    xi_ref[...] = _dot(ub, (bre * ki + bim * kr).astype(BF16))

    for c in range(S5_STATES // lane_blk):
        sl = slice(c * lane_blk, (c + 1) * lane_blk)
        ar = jnp.broadcast_to(ar_ref[:, sl], (nb, lane_blk))
        ai = jnp.broadcast_to(ai_ref[:, sl], (nb, lane_blk))

        def step(t, carry):
            sr, si = carry
            row = pl.ds(pl.multiple_of(t * nb, nb), nb)
            nr = ar * sr - ai * si + xr_ref[row, sl]
            ni = ar * si + ai * sr + xi_ref[row, sl]
            xr_ref[row, sl] = nr
            xi_ref[row, sl] = ni
            return nr, ni

        sr, si = lax.fori_loop(0, tc, step, (sr_ref[:, sl], si_ref[:, sl]), unroll=4)
        sr_ref[:, sl] = sr
        si_ref[:, sl] = si

    y = (_dot(xr_ref[...].astype(BF16), cre_ref[...]) - _dot(xi_ref[...].astype(BF16), cim_ref[...])
         + d_ref[...] * u)
    y_ref[...] = y.reshape(tc, nb, S5_WIDTH)


def _s5_scan(u_t, bre, bim, kr, ki, ar, ai, cre, cim, d, tc=64, lane_blk=256):
    S, B, W = u_t.shape
    const = lambda shape: pl.BlockSpec(shape, lambda i: (0, 0))
    return pl.pallas_call(
        functools.partial(_s5_kernel, tc=tc, nb=B, lane_blk=lane_blk),
        grid=(S // tc,),
        in_specs=[
            pl.BlockSpec((tc, B, W), lambda i: (i, 0, 0)),
            const((W, S5_STATES)), const((W, S5_STATES)),
            const((1, S5_STATES)), const((1, S5_STATES)), const((1, S5_STATES)), const((1, S5_STATES)),
            const((S5_STATES, W)), const((S5_STATES, W)), const((1, W)),
        ],
        out_specs=pl.BlockSpec((tc, B, W), lambda i: (i, 0, 0)),
        out_shape=jax.ShapeDtypeStruct((S, B, W), F32),
        scratch_shapes=[
            pltpu.VMEM((tc * B, S5_STATES), F32), pltpu.VMEM((tc * B, S5_STATES), F32),
            pltpu.VMEM((B, S5_STATES), F32), pltpu.VMEM((B, S5_STATES), F32),
        ],
        compiler_params=_params(("arbitrary",)),
        name="s5_scan",
    )(u_t, bre, bim, kr, ki, ar, ai, cre, cim, d)


def _mix_out_kernel(h_ref, om_ref, on_ref, of_ref, ys_ref, gw_ref, gb_ref, gg_ref, wo_ref, pg_ref, o_ref):
    gg = gg_ref[0]
    acc = jnp.zeros(h_ref.shape[1:], F32)
    for gi, ref in enumerate((om_ref, on_ref, of_ref)):
        parts = [ref[0, hh] for hh in range(4)]
        ss = sum(jnp.sum(p * p, axis=-1, keepdims=True) for p in parts)
        rinv = lax.rsqrt(ss * (1.0 / GROUP_WIDTH) + NORM_EPS)
        for hh, p in enumerate(parts):
            r0 = gi * GROUP_WIDTH + hh * HEAD_DIM
            gcol = gg[gi:gi + 1, hh * HEAD_DIM:(hh + 1) * HEAD_DIM]
            acc = acc + _dot((p * rinv * gcol).astype(BF16), wo_ref[0, r0:r0 + HEAD_DIM, :])
    y = jax.nn.gelu(ys_ref[0])
    y = y * jax.nn.sigmoid(_dot(y.astype(BF16), gw_ref[0]) + gb_ref[...])
    acc = acc + _dot(_rms(y, gg[3:4, :]).astype(BF16), wo_ref[0, 3 * GROUP_WIDTH:, :])
    o_ref[0] = h_ref[0] + _rms(acc, pg_ref[...])


def _mix_out(h, o_mla, o_nsa, o_fox, y_s5, glu_w, glu_b, group_g, w_out, post_g, layer, tm=512):
    B, S, D = h.shape

    def heads():
        return pl.BlockSpec((1, 4, tm, HEAD_DIM), lambda b, t: (b, 0, t, 0))

    return pl.pallas_call(
        _mix_out_kernel,
        grid=(B, S // tm),
        in_specs=[
            pl.BlockSpec((1, tm, D), lambda b, t: (b, t, 0)),
            heads(), heads(), heads(),
            pl.BlockSpec((1, tm, S5_WIDTH), lambda b, t: (b, t, 0)),
            pl.BlockSpec((1, S5_WIDTH, S5_WIDTH), lambda b, t: (layer, 0, 0)),
            pl.BlockSpec((None, 1, S5_WIDTH), lambda b, t: (layer, 0, 0)),
            pl.BlockSpec((1, 4, GROUP_WIDTH), lambda b, t: (layer, 0, 0)),
            pl.BlockSpec((1, D, D), lambda b, t: (layer, 0, 0)),
            pl.BlockSpec((None, 1, D), lambda b, t: (layer, 0, 0)),
        ],
        out_specs=pl.BlockSpec((1, tm, D), lambda b, t: (b, t, 0)),
        out_shape=jax.ShapeDtypeStruct((B, S, D), F32),
        compiler_params=_params(("parallel", "parallel")),
        name="mix_out",
    )(h, o_mla, o_nsa, o_fox, y_s5, glu_w, glu_b, group_g, w_out, post_g)


def _xa_kv_kernel(mem_ref, g_ref, wkv_ref, kv_ref):
    m = _rms(mem_ref[0], g_ref[...]).astype(BF16)
    kv_ref[0] = _dot(m, wkv_ref[0]).astype(BF16)


def _xa_kv(mem, g, wkv, layer):
    B, M, D = mem.shape
    return pl.pallas_call(
        _xa_kv_kernel,
        grid=(B,),
        in_specs=[
            pl.BlockSpec((1, M, D), lambda b: (b, 0, 0)),
            pl.BlockSpec((None, 1, D), lambda b: (layer, 0, 0)),
            pl.BlockSpec((1, D, 2 * D), lambda b: (layer, 0, 0)),
        ],
        out_specs=pl.BlockSpec((1, M, 2 * D), lambda b: (b, 0, 0)),
        out_shape=jax.ShapeDtypeStruct((B, M, 2 * D), BF16),
        compiler_params=_params(("parallel",)),
        name="xa_kv",
    )(mem, g, wkv)


def _xa_kernel(h_ref, kv_ref, pre_g_ref, wq_ref, wo_ref, post_g_ref, o_ref):
    h = h_ref[0]
    u = _rms(h, pre_g_ref[...]).astype(BF16)
    q = _dot(u, wq_ref[0]) * (XA_HEAD_DIM ** -0.5)
    acc = jnp.zeros(h.shape, F32)
    for hh in range(XA_HEADS):
        sl = slice(hh * XA_HEAD_DIM, (hh + 1) * XA_HEAD_DIM)
        k = kv_ref[0, :, sl]
        v = kv_ref[0, :, D_MODEL + hh * XA_HEAD_DIM:D_MODEL + (hh + 1) * XA_HEAD_DIM]
        s = _dot_nt(q[:, sl].astype(BF16), k)
        m = jnp.max(s, axis=-1, keepdims=True)
        e = jnp.exp(s - m)
        p = e / jnp.sum(e, axis=-1, keepdims=True)
        o = _dot(p.astype(BF16), v)
        acc = acc + _dot(o.astype(BF16), wo_ref[0, sl, :])
    o_ref[0] = h + _rms(acc, post_g_ref[...])


def _xa(h, kv, pre_g, wq, wo, post_g, layer, tm=512):
    B, S, D = h.shape
    M = kv.shape[1]
    return pl.pallas_call(
        _xa_kernel,
        grid=(B, S // tm),
        in_specs=[
            pl.BlockSpec((1, tm, D), lambda b, t: (b, t, 0)),
            pl.BlockSpec((1, M, 2 * D), lambda b, t: (b, 0, 0)),
            pl.BlockSpec((None, 1, D), lambda b, t: (layer, 0, 0)),
            pl.BlockSpec((1, D, D), lambda b, t: (layer, 0, 0)),
            pl.BlockSpec((1, D, D), lambda b, t: (layer, 0, 0)),
            pl.BlockSpec((None, 1, D), lambda b, t: (layer, 0, 0)),
        ],
        out_specs=pl.BlockSpec((1, tm, D), lambda b, t: (b, t, 0)),
        out_shape=jax.ShapeDtypeStruct((B, S, D), F32),
        compiler_params=_params(("parallel", "parallel")),
        name="cross_attn",
    )(h, kv, pre_g, wq, wo, post_g)


def _rope_tables(n, dim):
    inv = 1.0 / (ROPE_THETA ** (jnp.arange(0, dim, 2, dtype=F32) / dim))
    ang = jnp.arange(n, dtype=F32)[:, None] * inv[None, :]
    return jnp.cos(ang), jnp.sin(ang)


def _pack_w_in(w):
    L, D, _ = w.shape
    z = lambda n: jnp.zeros((L, D, n), w.dtype)
    o = 0
    cuts = {}
    for name, n in (("cq", MLA_Q_LORA), ("ckv", MLA_KV_LORA), ("kpe", MLA_ROPE_DIM), ("nq", 256), ("kc", 64),
                    ("vc", 64), ("ksl", 64), ("vsl", 64), ("kw", 64), ("vw", 64), ("gn", N_GATES),
                    ("fq", 256), ("fk", 256), ("fv", 256), ("ff", FOX_HEADS), ("s5", S5_WIDTH)):
        cuts[name] = w[:, :, o:o + n]
        o += n
    c = cuts
    return jnp.concatenate([
        c["cq"], z(64), c["ckv"], z(MLA_NOPE_DIM), c["kpe"], z(32), c["nq"], c["kc"], c["vc"], c["ksl"], c["vsl"],
        c["kw"], c["vw"], c["gn"], c["ff"], z(LANES - N_GATES - FOX_HEADS), c["fq"], c["fk"], c["fv"], c["s5"],
    ], axis=-1)


def kernel(x, mem, ffn1_pre_g, ffn1_w_gate, ffn1_w_up, ffn1_w_down, ffn1_post_g, mix_pre_g, mix_w_in, mla_q_norm_g, mla_w_uq, mla_kv_norm_g, mla_w_ukv, nsa_cmp_pos, nsa_phi_w1, nsa_phi_b1, nsa_phi_w2, nsa_gate_b, fox_f_b, s5_lambda_re, s5_lambda_im, s5_log_dt, s5_b_re, s5_b_im, s5_c_re, s5_c_im, s5_d, s5_glu_w, s5_glu_b, mix_group_g, mix_w_out, mix_post_g, xa_pre_g, xa_mem_g, xa_w_q, xa_w_kv, xa_w_o, xa_post_g, ffn2_pre_g, ffn2_w_gate, ffn2_w_up, ffn2_w_down, ffn2_post_g):
    B, S, D = x.shape
    L = ffn1_pre_g.shape[0]
    T = B * S
    G, P, C = S5_GROUPS, S5_STATE, S5_GROUP
    bf = lambda a: a.astype(BF16)

    w_in = bf(_pack_w_in(mix_w_in))
    qg = jnp.pad(mla_q_norm_g, ((0, 0), (0, 256 - MLA_Q_LORA)))
    wuq = mla_w_uq.reshape(L, MLA_Q_LORA, MLA_HEADS, MLA_NOPE_DIM + MLA_ROPE_DIM)
    wuq = jnp.pad(wuq, ((0, 0), (0, 256 - MLA_Q_LORA), (0, 0), (0, MLA_PAD_DIM - MLA_NOPE_DIM - MLA_ROPE_DIM)))
    wuq = bf(wuq.reshape(L, 256, MLA_HEADS * MLA_PAD_DIM))
    wukv = mla_w_ukv.reshape(L, MLA_KV_LORA, MLA_HEADS, MLA_NOPE_DIM + MLA_V_DIM)
    wk = jnp.pad(wukv[..., :MLA_NOPE_DIM], ((0, 0), (0, 0), (0, 0), (0, MLA_PAD_DIM - MLA_NOPE_DIM)))
    wukv = bf(jnp.concatenate([wk.reshape(L, MLA_KV_LORA, -1), wukv[..., MLA_NOPE_DIM:].reshape(L, MLA_KV_LORA, -1)], axis=-1))
    misc_bias = jnp.concatenate([nsa_gate_b, fox_f_b, jnp.zeros((L, LANES - N_GATES - FOX_HEADS), F32)], axis=-1)

    c32, s32 = _rope_tables(S, HEAD_DIM)
    cos64 = jnp.tile(jnp.concatenate([c32, c32], axis=-1), (1, 2))
    sin64 = jnp.tile(jnp.concatenate([-s32, s32], axis=-1), (1, 2))
    c16, s16 = _rope_tables(S, MLA_ROPE_DIM)
    one, zero = jnp.ones((S, 1), F32), jnp.zeros((S, 1), F32)
    cosm = jnp.concatenate([jnp.tile(one, (1, 64)), c16, c16, jnp.tile(one, (1, 32))], axis=-1)
    sinm = jnp.concatenate([jnp.tile(zero, (1, 64)), -s16, s16, jnp.tile(zero, (1, 32))], axis=-1)

    cmp_pos = nsa_cmp_pos.reshape(L * 2, 1, NSA_CMP_LEN * HEAD_DIM)
    phi_w1 = bf(nsa_phi_w1.reshape(L * 2, NSA_CMP_LEN * HEAD_DIM, NSA_CMP_HIDDEN))
    phi_b1 = nsa_phi_b1.reshape(L * 2, 1, NSA_CMP_HIDDEN)
    phi_w2 = bf(nsa_phi_w2.reshape(L * 2, NSA_CMP_HIDDEN, HEAD_DIM))

    n_cmp_pad = S // NSA_CMP_STRIDE
    n_blk = S // NSA_SEL_LEN
    cs = np.arange(n_cmp_pad) * NSA_CMP_STRIDE
    ss = np.arange(LANES) * NSA_SEL_LEN
    ovl_np = ((cs[:, None] < ss[None, :] + NSA_SEL_LEN) & (cs[:, None] + NSA_CMP_LEN > ss[None, :])
              & (np.arange(LANES)[None, :] < n_blk) & (np.arange(n_cmp_pad)[:, None] < n_cmp_pad - 1))
    ovl = jnp.asarray(ovl_np.astype(np.float32), dtype=BF16)
    tq_nsa = 256
    key_blk = (np.arange(S) // NSA_SEL_LEN).reshape(S // tq_nsa, 1, tq_nsa)
    expand = jnp.asarray((key_blk == np.arange(LANES)[None, :, None]).astype(np.float32), dtype=BF16)

    dt = jnp.exp(s5_log_dt)[:, :, None]
    lr, li = s5_lambda_re, s5_lambda_im
    mag = jnp.exp(lr * dt)
    a_r, a_i = mag * jnp.cos(li * dt), mag * jnp.sin(li * dt)
    den = lr * lr + li * li
    k_r = ((a_r - 1.0) * lr + a_i * li) / den
    k_i = (a_i * lr - (a_r - 1.0) * li) / den
    flat = lambda a: a.reshape(L, 1, G * P)
    a_r, a_i, k_r, k_i = flat(a_r), flat(a_i), flat(k_r), flat(k_i)
    eye = jnp.eye(G, dtype=F32)
    blk_b = lambda b: jnp.einsum('lgpc,gh->lgchp', b, eye).reshape(L, G * C, G * P)
    blk_c = lambda c: jnp.einsum('lgcp,gh->lgphc', c, eye).reshape(L, G * P, G * C)
    b_re, b_im = blk_b(s5_b_re), blk_b(s5_b_im)
    c_re, c_im = bf(blk_c(s5_c_re)), bf(blk_c(s5_c_im))
    d_skip = s5_d.reshape(L, 1, S5_WIDTH)

    ffn_w = [(bf(ffn1_w_gate), bf(ffn1_w_up), bf(ffn1_w_down)), (bf(ffn2_w_gate), bf(ffn2_w_up), bf(ffn2_w_down))]
    glu_w, w_out = bf(s5_glu_w), bf(mix_w_out)
    wq_x, wkv_x, wo_x = bf(xa_w_q), bf(xa_w_kv), bf(xa_w_o)

    v3 = lambda a: a.reshape(L, 1, a.shape[-1])
    ffn1_pre_g, ffn1_post_g, ffn2_pre_g, ffn2_post_g = v3(ffn1_pre_g), v3(ffn1_post_g), v3(ffn2_pre_g), v3(ffn2_post_g)
    mix_pre_g, mix_post_g, qg, kvg, misc_bias = v3(mix_pre_g), v3(mix_post_g), v3(qg), v3(mla_kv_norm_g), v3(misc_bias)
    s5_glu_b, xa_mem_g, xa_pre_g, xa_post_g = v3(s5_glu_b), v3(xa_mem_g), v3(xa_pre_g), v3(xa_post_g)

    h = x
    for l in range(L):
        h = _ffn(h.reshape(T, D), ffn1_pre_g, *ffn_w[0], ffn1_post_g, l).reshape(B, S, D)

        (mq, mk, mv, nq, nkc, nvc, nksl, nvsl, nkw, nvw, misc, fq, fk, fv, s5u) = _proj(
            h, mix_pre_g, w_in, qg, wuq, kvg, wukv, misc_bias, cos64, sin64, cosm, sinm, l)

        o_mla = _causal_attn(mq, mk, mv)

        cum = _cumsum(misc)[:, :, N_GATES:N_GATES + FOX_HEADS]
        cum_t = jnp.transpose(cum, (0, 2, 1))
        o_fox = _causal_attn(fq, fk, fv, cum_t[..., None], cum_t[:, :, None, :])

        t2 = jnp.stack([nkc, nvc]).reshape(2, B, n_cmp_pad, NSA_CMP_STRIDE * HEAD_DIM)
        cmp = _nsa_compress(t2, cmp_pos, phi_w1, phi_b1, phi_w2, l)
        o_nsa = _nsa_attn(nq, cmp, nksl, nvsl, nkw, nvw, misc, ovl, expand, tq=tq_nsa)

        y_s5 = _s5_scan(jnp.transpose(s5u, (1, 0, 2)), b_re[l], b_im[l], k_r[l], k_i[l], a_r[l], a_i[l],
                        c_re[l], c_im[l], d_skip[l])
        y_s5 = jnp.transpose(y_s5, (1, 0, 2))

        h = _mix_out(h, o_mla, o_nsa, o_fox, y_s5, glu_w, s5_glu_b, mix_group_g, w_out, mix_post_g, l)

        kv = _xa_kv(mem, xa_mem_g, wkv_x, l)
        h = _xa(h, kv, xa_pre_g, wq_x, wo_x, xa_post_g, l)

        h = _ffn(h.reshape(T, D), ffn2_pre_g, *ffn_w[1], ffn2_post_g, l).reshape(B, S, D)
    return h
```

```python
import functools
import math

import numpy as np
import jax
import jax.numpy as jnp
from jax import lax
from jax.experimental import pallas as pl
from jax.experimental.pallas import tpu as pltpu

F32 = jnp.float32
BF16 = jnp.bfloat16

D_MODEL = 1024
HEAD_DIM = 64
ROPE_THETA = 10000.0
NORM_EPS = 1e-6
NEG_INF = -1e30
MACARON_WEIGHT = 0.5
FFN_DIM = 2816
XA_HEADS = 4
XA_HEAD_DIM = D_MODEL // XA_HEADS

MLA_HEADS = 4
MLA_Q_LORA = 192
MLA_KV_LORA = 128
MLA_NOPE_DIM = 64
MLA_ROPE_DIM = 32
MLA_V_DIM = 64
MLA_PAD_DIM = 128

NSA_HEADS = 4
NSA_CMP_LEN = 32
NSA_CMP_STRIDE = 16
NSA_CMP_HIDDEN = 256
NSA_SEL_LEN = 64
NSA_SEL_BLOCKS = 8
NSA_WINDOW = 256
NSA_FORCE_SCORE = 1e4

FOX_HEADS = 4

S5_WIDTH = 256
S5_GROUP = 16
S5_GROUPS = S5_WIDTH // S5_GROUP
S5_STATE = 64
S5_STATES = S5_GROUPS * S5_STATE

GROUP_WIDTH = 256
LANES = 128
VMEM_LIMIT = 56 * 1024 * 1024

SEG_CQ, SEG_CKV, SEG_KPE, SEG_NQ = 0, 256, 384, 512
SEG_NCMP, SEG_NSLC, SEG_NSWA, SEG_MISC = 768, 896, 1024, 1152
SEG_FQ, SEG_FK, SEG_FV, SEG_S5, IN_PACKED = 1280, 1536, 1792, 2048, 2304
N_GATES = 3 * NSA_HEADS


def _params(sem):
    return pltpu.CompilerParams(dimension_semantics=sem, vmem_limit_bytes=VMEM_LIMIT)


def _rms(x, g, n=None):
    n = x.shape[-1] if n is None else n
    ms = jnp.sum(x * x, axis=-1, keepdims=True) * (1.0 / n)
    return x * lax.rsqrt(ms + NORM_EPS) * g


def _dot(a, b):
    return jnp.dot(a, b, preferred_element_type=F32)


def _dot_nt(a, b):
    return lax.dot_general(a, b, (((1,), (1,)), ((), ())), preferred_element_type=F32)


def _split3(x):
    hi = x.astype(BF16)
    r = x - hi.astype(F32)
    mid = r.astype(BF16)
    lo = (r - mid.astype(F32)).astype(BF16)
    return hi, mid, lo


def _masked_softmax(s, mask):
    s = jnp.where(mask, s, NEG_INF)
    m = jnp.max(s, axis=-1, keepdims=True)
    e = jnp.where(mask, jnp.exp(s - m), 0.0)
    return e / jnp.maximum(jnp.sum(e, axis=-1, keepdims=True), 1e-30)


def _ffn_kernel(h_ref, pre_g_ref, wg_ref, wu_ref, wd_ref, post_g_ref, o_ref, *, chunk):
    h = h_ref[...]
    u = _rms(h, pre_g_ref[...]).astype(BF16)
    acc = jnp.zeros(h.shape, F32)
    for c in range(wg_ref.shape[1] // chunk):
        sl = slice(c * chunk, (c + 1) * chunk)
        g = _dot(u, wg_ref[:, sl])
        p = _dot(u, wu_ref[:, sl])
        a = (g * jax.nn.sigmoid(g)) * p
        acc = acc + _dot(a.astype(BF16), wd_ref[sl, :])
    o_ref[...] = h + MACARON_WEIGHT * _rms(acc, post_g_ref[...])


def _ffn(h, pre_g, wg, wu, wd, post_g, layer, tm=512, chunk=256):
    T, D = h.shape
    F = wg.shape[-1]
    wspec = functools.partial(pl.BlockSpec, pipeline_mode=pl.Buffered(1))
    return pl.pallas_call(
        functools.partial(_ffn_kernel, chunk=chunk),
        grid=(T // tm,),
        in_specs=[
            pl.BlockSpec((tm, D), lambda i: (i, 0)),
            pl.BlockSpec((None, 1, D), lambda i: (layer, 0, 0)),
            wspec((None, D, F), lambda i: (layer, 0, 0)),
            wspec((None, D, F), lambda i: (layer, 0, 0)),
            wspec((None, F, D), lambda i: (layer, 0, 0)),
            pl.BlockSpec((None, 1, D), lambda i: (layer, 0, 0)),
        ],
        out_specs=pl.BlockSpec((tm, D), lambda i: (i, 0)),
        out_shape=jax.ShapeDtypeStruct((T, D), F32),
        compiler_params=_params(("parallel",)),
        name="ffn",
    )(h, pre_g, wg, wu, wd, post_g)


def _proj_kernel(h_ref, g_ref, win_ref, qg_ref, wuq_ref, kvg_ref, wukv_ref, bias_ref,
                 cos64_ref, sin64_ref, cosm_ref, sinm_ref,
                 mq_ref, mk_ref, mv_ref, nq_ref, nkc_ref, nvc_ref, nksl_ref, nvsl_ref, nkw_ref, nvw_ref,
                 misc_ref, fq_ref, fk_ref, fv_ref, s5_ref):
    h = h_ref[0]
    tm = h.shape[0]
    u = _rms(h, g_ref[...]).astype(BF16)
    z = _dot(u, win_ref[...])
    lane = lax.broadcasted_iota(jnp.int32, (tm, LANES), 1)

    def rope(x, c, s, half):
        fwd = pltpu.roll(x, LANES - half, 1)
        bwd = pltpu.roll(x, half, 1)
        return x * c + jnp.where(lane % (2 * half) < half, fwd, bwd) * s

    cos64, sin64 = cos64_ref[...], sin64_ref[...]
    cosm, sinm = cosm_ref[...], sinm_ref[...]
    half_m = MLA_ROPE_DIM // 2
    half_n = HEAD_DIM // 2

    cq = _rms(z[:, SEG_CQ:SEG_CKV], qg_ref[...], MLA_Q_LORA).astype(BF16)
    q = _dot(cq, wuq_ref[...])
    ckv = _rms(z[:, SEG_CKV:SEG_KPE], kvg_ref[...]).astype(BF16)
    kv = _dot(ckv, wukv_ref[...])
    kpe = rope(z[:, SEG_KPE:SEG_NQ], cosm, sinm, half_m)
    mla_scale = (MLA_NOPE_DIM + MLA_ROPE_DIM) ** -0.5
    for hh in range(MLA_HEADS):
        sl = slice(hh * MLA_PAD_DIM, (hh + 1) * MLA_PAD_DIM)
        mq_ref[0, hh] = (rope(q[:, sl], cosm, sinm, half_m) * mla_scale).astype(BF16)
        mk_ref[0, hh] = (kv[:, sl] + kpe).astype(BF16)
        v0 = MLA_HEADS * MLA_PAD_DIM + hh * MLA_V_DIM
        mv_ref[0, hh] = kv[:, v0:v0 + MLA_V_DIM].astype(BF16)

    scale = HEAD_DIM ** -0.5
    for c in range(NSA_HEADS // 2):
        r = rope(z[:, SEG_NQ + c * LANES:SEG_NQ + (c + 1) * LANES], cos64, sin64, half_n) * scale
        nq_ref[0, 2 * c] = r[:, :HEAD_DIM].astype(BF16)
        nq_ref[0, 2 * c + 1] = r[:, HEAD_DIM:].astype(BF16)
    seg = z[:, SEG_NCMP:SEG_NSLC]
    nkc_ref[0] = rope(seg, cos64, sin64, half_n)[:, :HEAD_DIM]
    nvc_ref[0] = seg[:, HEAD_DIM:]
    seg = z[:, SEG_NSLC:SEG_NSWA]
    nksl_ref[0] = rope(seg, cos64, sin64, half_n)[:, :HEAD_DIM].astype(BF16)
    nvsl_ref[0] = seg[:, HEAD_DIM:].astype(BF16)
    seg = z[:, SEG_NSWA:SEG_MISC]
    nkw_ref[0] = rope(seg, cos64, sin64, half_n)[:, :HEAD_DIM].astype(BF16)
    nvw_ref[0] = seg[:, HEAD_DIM:].astype(BF16)

    x = z[:, SEG_MISC:SEG_FQ] + bias_ref[...]
    log_sig = jnp.minimum(x, 0.0) - jnp.log(1.0 + jnp.exp(-jnp.abs(x)))
    misc_ref[0] = jnp.where(lane < N_GATES, jax.nn.sigmoid(x), log_sig)

    for hh in range(FOX_HEADS):
        sl = slice(hh * HEAD_DIM, (hh + 1) * HEAD_DIM)
        fq_ref[0, hh] = (z[:, SEG_FQ:SEG_FK][:, sl] * scale).astype(BF16)
        fk_ref[0, hh] = z[:, SEG_FK:SEG_FV][:, sl].astype(BF16)
        fv_ref[0, hh] = z[:, SEG_FV:SEG_S5][:, sl].astype(BF16)

    s5_ref[0] = z[:, SEG_S5:IN_PACKED]


def _proj(h, g, win, qg, wuq, kvg, wukv, bias, cos64, sin64, cosm, sinm, layer, tm=512):
    B, S, D = h.shape
    nt = S // tm
    wspec = functools.partial(pl.BlockSpec, pipeline_mode=pl.Buffered(1))

    def w2(shape):
        return wspec((None,) + shape, lambda b, t: (layer, 0, 0))

    def vec(n):
        return pl.BlockSpec((None, 1, n), lambda b, t: (layer, 0, 0))

    def tab():
        return pl.BlockSpec((tm, LANES), lambda b, t: (t, 0))

    def heads(d):
        return pl.BlockSpec((1, 4, tm, d), lambda b, t: (b, 0, t, 0))

    def tok(d):
        return pl.BlockSpec((1, tm, d), lambda b, t: (b, t, 0))

    def hshape(d, dt=BF16):
        return jax.ShapeDtypeStruct((B, 4, S, d), dt)

    def tshape(d, dt):
        return jax.ShapeDtypeStruct((B, S, d), dt)

    return pl.pallas_call(
        _proj_kernel,
        grid=(B, nt),
        in_specs=[
            tok(D), vec(D), w2((D, IN_PACKED)), vec(256), w2((256, 4 * MLA_PAD_DIM)),
            vec(MLA_KV_LORA), w2((MLA_KV_LORA, 4 * MLA_PAD_DIM + 4 * MLA_V_DIM)), vec(LANES),
            tab(), tab(), tab(), tab(),
        ],
        out_specs=[
            heads(MLA_PAD_DIM), heads(MLA_PAD_DIM), heads(MLA_V_DIM),
            heads(HEAD_DIM), tok(HEAD_DIM), tok(HEAD_DIM), tok(HEAD_DIM), tok(HEAD_DIM), tok(HEAD_DIM), tok(HEAD_DIM),
            tok(LANES), heads(HEAD_DIM), heads(HEAD_DIM), heads(HEAD_DIM), tok(S5_WIDTH),
        ],
        out_shape=[
            hshape(MLA_PAD_DIM), hshape(MLA_PAD_DIM), hshape(MLA_V_DIM),
            hshape(HEAD_DIM), tshape(HEAD_DIM, F32), tshape(HEAD_DIM, F32),
            tshape(HEAD_DIM, BF16), tshape(HEAD_DIM, BF16), tshape(HEAD_DIM, BF16), tshape(HEAD_DIM, BF16),
            tshape(LANES, F32), hshape(HEAD_DIM), hshape(HEAD_DIM), hshape(HEAD_DIM), tshape(S5_WIDTH, F32),
        ],
        compiler_params=_params(("parallel", "parallel")),
        name="mix_proj",
    )(h, g, win, qg, wuq, kvg, wukv, bias, cos64, sin64, cosm, sinm)


def _cumsum_kernel(x_ref, o_ref, *, blk):
    S = x_ref.shape[1]
    r = lax.broadcasted_iota(jnp.int32, (blk, blk), 0)
    c = lax.broadcasted_iota(jnp.int32, (blk, blk), 1)
    tri = jnp.where(r >= c, 1.0, 0.0).astype(BF16)
    carry = jnp.zeros((1, x_ref.shape[2]), F32)
    for i in range(S // blk):
        hi, mid, lo = _split3(x_ref[0, i * blk:(i + 1) * blk, :])
        cs = _dot(tri, hi) + _dot(tri, mid) + _dot(tri, lo) + carry
        o_ref[0, i * blk:(i + 1) * blk, :] = cs
        carry = cs[blk - 1:blk, :]


def _cumsum(x, blk=256):
    B, S, W = x.shape
    return pl.pallas_call(
        functools.partial(_cumsum_kernel, blk=blk),
        grid=(B,),
        in_specs=[pl.BlockSpec((1, S, W), lambda b: (b, 0, 0))],
        out_specs=pl.BlockSpec((1, S, W), lambda b: (b, 0, 0)),
        out_shape=jax.ShapeDtypeStruct((B, S, W), F32),
        compiler_params=_params(("parallel",)),
        name="fox_cumsum",
    )(x)


def _causal_attn_kernel(*refs, tq, has_bias):
    if has_bias:
        q_ref, k_ref, v_ref, cq_ref, ck_ref, o_ref = refs
    else:
        q_ref, k_ref, v_ref, o_ref = refs
    S = q_ref.shape[2]
    r = lax.broadcasted_iota(jnp.int32, (tq, tq), 0)
    c = lax.broadcasted_iota(jnp.int32, (tq, tq), 1)
    tri = r >= c
    for i in range(S // tq):
        lo, hi = i * tq, (i + 1) * tq
        q = q_ref[0, 0, lo:hi, :]
        s_d = _dot_nt(q, k_ref[0, 0, lo:hi, :])
        if has_bias:
            cq = cq_ref[0, 0, lo:hi, :]
            s_d = s_d + cq - ck_ref[0, 0, :, lo:hi]
        s_d = jnp.where(tri, s_d, NEG_INF)
        m = jnp.max(s_d, axis=-1, keepdims=True)
        if i > 0:
            s_o = _dot_nt(q, k_ref[0, 0, :lo, :])
            if has_bias:
                s_o = s_o + cq - ck_ref[0, 0, :, :lo]
            m = jnp.maximum(m, jnp.max(s_o, axis=-1, keepdims=True))
        p_d = jnp.where(tri, jnp.exp(s_d - m), 0.0)
        l = jnp.sum(p_d, axis=-1, keepdims=True)
        acc = _dot(p_d.astype(BF16), v_ref[0, 0, lo:hi, :])
        if i > 0:
            p_o = jnp.exp(s_o - m)
            l = l + jnp.sum(p_o, axis=-1, keepdims=True)
            acc = acc + _dot(p_o.astype(BF16), v_ref[0, 0, :lo, :])
        o_ref[0, 0, lo:hi, :] = acc / jnp.maximum(l, 1e-30)


def _causal_attn(q, k, v, cum_col=None, cum_row=None, tq=256):
    B, H, S, dk = q.shape
    dv = v.shape[-1]
    has_bias = cum_col is not None

    def full(w):
        return pl.BlockSpec((1, 1, S, w), lambda b, h: (b, h, 0, 0))

    in_specs = [full(dk), full(dk), full(dv)]
    args = [q, k, v]
    if has_bias:
        in_specs += [full(1), pl.BlockSpec((1, 1, 1, S), lambda b, h: (b, h, 0, 0))]
        args += [cum_col, cum_row]
    return pl.pallas_call(
        functools.partial(_causal_attn_kernel, tq=tq, has_bias=has_bias),
        grid=(B, H),
        in_specs=in_specs,
        out_specs=full(dv),
        out_shape=jax.ShapeDtypeStruct((B, H, S, dv), F32),
        compiler_params=_params(("parallel", "parallel")),
        name="fox_attn" if has_bias else "mla_attn",
    )(*args)


def _nsa_compress_kernel(t_ref, pos_ref, w1_ref, b1_ref, w2_ref, o_ref):
    t = t_ref[0, 0]
    half = t.shape[1]
    a = _dot((t + pos_ref[0, :, :half]).astype(BF16), w1_ref[0, :half, :])
    b = _dot((t + pos_ref[0, :, half:]).astype(BF16), w1_ref[0, half:, :])
    nrow = t.shape[0]
    hid = jax.nn.gelu(a + pltpu.roll(b, nrow - 1, 0) + b1_ref[0])
    o_ref[0, 0] = _dot(hid.astype(BF16), w2_ref[0]).astype(BF16)


def _nsa_compress(t2, pos, w1, b1, w2, layer):
    _, B, R, W = t2.shape
    return pl.pallas_call(
        _nsa_compress_kernel,
        grid=(2, B),
        in_specs=[
            pl.BlockSpec((1, 1, R, W), lambda j, b: (j, b, 0, 0)),
            pl.BlockSpec((1, 1, 2 * W), lambda j, b: (2 * layer + j, 0, 0)),
            pl.BlockSpec((1, 2 * W, NSA_CMP_HIDDEN), lambda j, b: (2 * layer + j, 0, 0)),
            pl.BlockSpec((1, 1, NSA_CMP_HIDDEN), lambda j, b: (2 * layer + j, 0, 0)),
            pl.BlockSpec((1, NSA_CMP_HIDDEN, HEAD_DIM), lambda j, b: (2 * layer + j, 0, 0)),
        ],
        out_specs=pl.BlockSpec((1, 1, R, HEAD_DIM), lambda j, b: (j, b, 0, 0)),
        out_shape=jax.ShapeDtypeStruct((2, B, R, HEAD_DIM), BF16),
        compiler_params=_params(("parallel", "parallel")),
        name="nsa_compress",
    )(t2, pos, w1, b1, w2)


def _nsa_attn_kernel(q_ref, cmp_ref, ksl_ref, vsl_ref, kw_ref, vw_ref, gate_ref, ovl_ref, exp_ref, o_ref, slc_ref, *, tq):
    H = NSA_HEADS
    qi = pl.program_id(1)
    lo = qi * tq
    q4 = q_ref[0].reshape(H * tq, HEAD_DIM)
    pos_q = lo + lax.broadcasted_iota(jnp.int32, (tq, 1), 0)
    lane = lax.broadcasted_iota(jnp.int32, (1, LANES), 1)

    kc, vc = cmp_ref[0, 0], cmp_ref[1, 0]
    nc = kc.shape[0]
    cmp_end = lax.broadcasted_iota(jnp.int32, (1, nc), 1) * NSA_CMP_STRIDE + (NSA_CMP_LEN - 1)
    mask_c = (pos_q >= cmp_end)[None]
    p_c = _masked_softmax(_dot_nt(q4, kc).reshape(H, tq, nc), mask_c)
    o_cmp = _dot(p_c.reshape(H * tq, nc).astype(BF16), vc).reshape(H, tq, HEAD_DIM)

    p_sum = p_c[0] + p_c[1] + p_c[2] + p_c[3]
    hi, mid, lo3 = _split3(p_sum)
    ovl = ovl_ref[...]
    imp = _dot(hi, ovl) + _dot(mid, ovl) + _dot(lo3, ovl)
    cur = pos_q // NSA_SEL_LEN
    forced = (lane == 0) | (lane == cur) | (lane == cur - 1)
    score = jnp.where(forced, NSA_FORCE_SCORE, imp)
    score = jnp.where(lane <= cur, score, NEG_INF)
    n_blk = exp_ref.shape[1] // NSA_SEL_LEN
    cnt = jnp.zeros(score.shape, F32)
    for kk in range(n_blk):
        col = score[:, kk:kk + 1]
        beats = (col > score) | ((col == score) & (lane > kk))
        cnt = cnt + jnp.where(beats, 1.0, 0.0)
    sel = jnp.where(cnt < min(NSA_SEL_BLOCKS, n_blk), 1.0, 0.0).astype(BF16)

    tri = lax.broadcasted_iota(jnp.int32, (tq, tq), 0) >= lax.broadcasted_iota(jnp.int32, (tq, tq), 1)

    def slc_variant(i):
        lo_k, hi_k = i * tq, (i + 1) * tq
        sel_k = _dot(sel, exp_ref[:, :hi_k])
        mask_d = ((sel_k[:, lo_k:] > 0.5) & tri)[None]
        s_d = jnp.where(mask_d, _dot_nt(q4, ksl_ref[0, lo_k:hi_k, :]).reshape(H, tq, tq), NEG_INF)
        m = jnp.max(s_d, axis=-1, keepdims=True)
        if i > 0:
            mask_o = (sel_k[:, :lo_k] > 0.5)[None]
            s_o = jnp.where(mask_o, _dot_nt(q4, ksl_ref[0, :lo_k, :]).reshape(H, tq, lo_k), NEG_INF)
            m = jnp.maximum(m, jnp.max(s_o, axis=-1, keepdims=True))
        p_d = jnp.where(mask_d, jnp.exp(s_d - m), 0.0)
        l = jnp.sum(p_d, axis=-1, keepdims=True)
        acc = _dot(p_d.reshape(H * tq, tq).astype(BF16), vsl_ref[0, lo_k:hi_k, :])
        if i > 0:
            p_o = jnp.where(mask_o, jnp.exp(s_o - m), 0.0)
            l = l + jnp.sum(p_o, axis=-1, keepdims=True)
            acc = acc + _dot(p_o.reshape(H * tq, lo_k).astype(BF16), vsl_ref[0, :lo_k, :])
        slc_ref[...] = acc.reshape(H, tq, HEAD_DIM) / jnp.maximum(l, 1e-30)

    for i in range(ksl_ref.shape[1] // tq):
        pl.when(qi == i)(functools.partial(slc_variant, i))
    o_slc = slc_ref[...]

    nw = NSA_WINDOW + tq
    start = pl.multiple_of(jnp.maximum(lo - NSA_WINDOW, 0), tq)
    kw = kw_ref[0, pl.ds(start, nw), :]
    vw = vw_ref[0, pl.ds(start, nw), :]
    kpos = start + lax.broadcasted_iota(jnp.int32, (1, nw), 1)
    mask_w = ((kpos <= pos_q) & (kpos > pos_q - NSA_WINDOW))[None]
    p_w = _masked_softmax(_dot_nt(q4, kw).reshape(H, tq, nw), mask_w)
    o_swa = _dot(p_w.reshape(H * tq, nw).astype(BF16), vw).reshape(H, tq, HEAD_DIM)

    g = gate_ref[0]
    for hh in range(H):
        o_ref[0, hh] = (g[:, hh:hh + 1] * o_cmp[hh] + g[:, H + hh:H + hh + 1] * o_slc[hh]
                        + g[:, 2 * H + hh:2 * H + hh + 1] * o_swa[hh])


def _nsa_attn(q, cmp, ksl, vsl, kw, vw, gates, ovl, expand, tq=256):
    B, H, S, d = q.shape
    R = cmp.shape[2]

    def full(w):
        return pl.BlockSpec((1, S, w), lambda b, i: (b, 0, 0))

    return pl.pallas_call(
        functools.partial(_nsa_attn_kernel, tq=tq),
        grid=(B, S // tq),
        in_specs=[
            pl.BlockSpec((1, H, tq, d), lambda b, i: (b, 0, i, 0)),
            pl.BlockSpec((2, 1, R, d), lambda b, i: (0, b, 0, 0)),
            full(d), full(d), full(d), full(d),
            pl.BlockSpec((1, tq, LANES), lambda b, i: (b, i, 0)),
            pl.BlockSpec((R, LANES), lambda b, i: (0, 0)),
            pl.BlockSpec((LANES, S), lambda b, i: (0, 0)),
        ],
        out_specs=pl.BlockSpec((1, H, tq, d), lambda b, i: (b, 0, i, 0)),
        out_shape=jax.ShapeDtypeStruct((B, H, S, d), F32),
        scratch_shapes=[pltpu.VMEM((H, tq, d), F32)],
        compiler_params=_params(("parallel", "arbitrary")),
        name="nsa_attn",
    )(q, cmp, ksl, vsl, kw, vw, gates, ovl, expand)


def _s5_kernel(u_ref, bre_ref, bim_ref, kr_ref, ki_ref, ar_ref, ai_ref, cre_ref, cim_ref, d_ref, y_ref,
               xr_ref, xi_ref, sr_ref, si_ref, *, tc, nb, lane_blk):
    @pl.when(pl.program_id(0) == 0)
    def _():
        sr_ref[...] = jnp.zeros(sr_ref.shape, F32)
        si_ref[...] = jnp.zeros(si_ref.shape, F32)

    u = u_ref[...].reshape(tc * nb, S5_WIDTH)
    ub = u.astype(BF16)
    kr, ki = kr_ref[...], ki_ref[...]
    bre, bim = bre_ref[...], bim_ref[...]
    xr_ref[...] = _dot(ub, (bre * kr - bim * ki).astype(BF16))
    xi_ref[...] = _dot(ub, (bre * ki + bim * kr).astype(BF16))

    for c in range(S5_STATES // lane_blk):
        sl = slice(c * lane_blk, (c + 1) * lane_blk)
        ar = jnp.broadcast_to(ar_ref[:, sl], (nb, lane_blk))
        ai = jnp.broadcast_to(ai_ref[:, sl], (nb, lane_blk))

        def step(t, carry):
            sr, si = carry
            row = pl.ds(pl.multiple_of(t * nb, nb), nb)
            nr = ar * sr - ai * si + xr_ref[row, sl]
            ni = ar * si + ai * sr + xi_ref[row, sl]
            xr_ref[row, sl] = nr
            xi_ref[row, sl] = ni
            return nr, ni

        sr, si = lax.fori_loop(0, tc, step, (sr_ref[:, sl], si_ref[:, sl]), unroll=4)
        sr_ref[:, sl] = sr
        si_ref[:, sl] = si

    y = (_dot(xr_ref[...].astype(BF16), cre_ref[...]) - _dot(xi_ref[...].astype(BF16), cim_ref[...])
         + d_ref[...] * u)
    y_ref[...] = y.reshape(tc, nb, S5_WIDTH)


def _s5_scan(u_t, bre, bim, kr, ki, ar, ai, cre, cim, d, tc=64, lane_blk=256):
    S, B, W = u_t.shape
    const = lambda shape: pl.BlockSpec(shape, lambda i: (0, 0))
    return pl.pallas_call(
        functools.partial(_s5_kernel, tc=tc, nb=B, lane_blk=lane_blk),
        grid=(S // tc,),
        in_specs=[
            pl.BlockSpec((tc, B, W), lambda i: (i, 0, 0)),
            const((W, S5_STATES)), const((W, S5_STATES)),
            const((1, S5_STATES)), const((1, S5_STATES)), const((1, S5_STATES)), const((1, S5_STATES)),
            const((S5_STATES, W)), const((S5_STATES, W)), const((1, W)),
        ],
        out_specs=pl.BlockSpec((tc, B, W), lambda i: (i, 0, 0)),
        out_shape=jax.ShapeDtypeStruct((S, B, W), F32),
        scratch_shapes=[
            pltpu.VMEM((tc * B, S5_STATES), F32), pltpu.VMEM((tc * B, S5_STATES), F32),
            pltpu.VMEM((B, S5_STATES), F32), pltpu.VMEM((B, S5_STATES), F32),
        ],
        compiler_params=_params(("arbitrary",)),
        name="s5_scan",
    )(u_t, bre, bim, kr, ki, ar, ai, cre, cim, d)


def _mix_out_kernel(h_ref, om_ref, on_ref, of_ref, ys_ref, gw_ref, gb_ref, gg_ref, wo_ref, pg_ref, o_ref):
    gg = gg_ref[0]
    acc = jnp.zeros(h_ref.shape[1:], F32)
    for gi, ref in enumerate((om_ref, on_ref, of_ref)):
        parts = [ref[0, hh] for hh in range(4)]
        ss = sum(jnp.sum(p * p, axis=-1, keepdims=True) for p in parts)
        rinv = lax.rsqrt(ss * (1.0 / GROUP_WIDTH) + NORM_EPS)
        for hh, p in enumerate(parts):
            r0 = gi * GROUP_WIDTH + hh * HEAD_DIM
            gcol = gg[gi:gi + 1, hh * HEAD_DIM:(hh + 1) * HEAD_DIM]
            acc = acc + _dot((p * rinv * gcol).astype(BF16), wo_ref[0, r0:r0 + HEAD_DIM, :])
    y = jax.nn.gelu(ys_ref[0])
    y = y * jax.nn.sigmoid(_dot(y.astype(BF16), gw_ref[0]) + gb_ref[...])
    acc = acc + _dot(_rms(y, gg[3:4, :]).astype(BF16), wo_ref[0, 3 * GROUP_WIDTH:, :])
    o_ref[0] = h_ref[0] + _rms(acc, pg_ref[...])


def _mix_out(h, o_mla, o_nsa, o_fox, y_s5, glu_w, glu_b, group_g, w_out, post_g, layer, tm=512):
    B, S, D = h.shape

    def heads():
        return pl.BlockSpec((1, 4, tm, HEAD_DIM), lambda b, t: (b, 0, t, 0))

    return pl.pallas_call(
        _mix_out_kernel,
        grid=(B, S // tm),
        in_specs=[
            pl.BlockSpec((1, tm, D), lambda b, t: (b, t, 0)),
            heads(), heads(), heads(),
            pl.BlockSpec((1, tm, S5_WIDTH), lambda b, t: (b, t, 0)),
            pl.BlockSpec((1, S5_WIDTH, S5_WIDTH), lambda b, t: (layer, 0, 0)),
            pl.BlockSpec((None, 1, S5_WIDTH), lambda b, t: (layer, 0, 0)),
            pl.BlockSpec((1, 4, GROUP_WIDTH), lambda b, t: (layer, 0, 0)),
            pl.BlockSpec((1, D, D), lambda b, t: (layer, 0, 0)),
            pl.BlockSpec((None, 1, D), lambda b, t: (layer, 0, 0)),
        ],
        out_specs=pl.BlockSpec((1, tm, D), lambda b, t: (b, t, 0)),
        out_shape=jax.ShapeDtypeStruct((B, S, D), F32),
        compiler_params=_params(("parallel", "parallel")),
        name="mix_out",
    )(h, o_mla, o_nsa, o_fox, y_s5, glu_w, glu_b, group_g, w_out, post_g)


def _xa_kv_kernel(mem_ref, g_ref, wkv_ref, kv_ref):
    m = _rms(mem_ref[0], g_ref[...]).astype(BF16)
    kv_ref[0] = _dot(m, wkv_ref[0]).astype(BF16)


def _xa_kv(mem, g, wkv, layer):
    B, M, D = mem.shape
    return pl.pallas_call(
        _xa_kv_kernel,
        grid=(B,),
        in_specs=[
            pl.BlockSpec((1, M, D), lambda b: (b, 0, 0)),
            pl.BlockSpec((None, 1, D), lambda b: (layer, 0, 0)),
            pl.BlockSpec((1, D, 2 * D), lambda b: (layer, 0, 0)),
        ],
        out_specs=pl.BlockSpec((1, M, 2 * D), lambda b: (b, 0, 0)),
        out_shape=jax.ShapeDtypeStruct((B, M, 2 * D), BF16),
        compiler_params=_params(("parallel",)),
        name="xa_kv",
    )(mem, g, wkv)


def _xa_kernel(h_ref, kv_ref, pre_g_ref, wq_ref, wo_ref, post_g_ref, o_ref):
    h = h_ref[0]
    u = _rms(h, pre_g_ref[...]).astype(BF16)
    q = _dot(u, wq_ref[0]) * (XA_HEAD_DIM ** -0.5)
    acc = jnp.zeros(h.shape, F32)
    for hh in range(XA_HEADS):
        sl = slice(hh * XA_HEAD_DIM, (hh + 1) * XA_HEAD_DIM)
        k = kv_ref[0, :, sl]
        v = kv_ref[0, :, D_MODEL + hh * XA_HEAD_DIM:D_MODEL + (hh + 1) * XA_HEAD_DIM]
        s = _dot_nt(q[:, sl].astype(BF16), k)
        m = jnp.max(s, axis=-1, keepdims=True)
        e = jnp.exp(s - m)
        p = e / jnp.sum(e, axis=-1, keepdims=True)
        o = _dot(p.astype(BF16), v)
        acc = acc + _dot(o.astype(BF16), wo_ref[0, sl, :])
    o_ref[0] = h + _rms(acc, post_g_ref[...])


def _xa(h, kv, pre_g, wq, wo, post_g, layer, tm=512):
    B, S, D = h.shape
    M = kv.shape[1]
    return pl.pallas_call(
        _xa_kernel,
        grid=(B, S // tm),
        in_specs=[
            pl.BlockSpec((1, tm, D), lambda b, t: (b, t, 0)),
            pl.BlockSpec((1, M, 2 * D), lambda b, t: (b, 0, 0)),
            pl.BlockSpec((None, 1, D), lambda b, t: (layer, 0, 0)),
            pl.BlockSpec((1, D, D), lambda b, t: (layer, 0, 0)),
            pl.BlockSpec((1, D, D), lambda b, t: (layer, 0, 0)),
            pl.BlockSpec((None, 1, D), lambda b, t: (layer, 0, 0)),
        ],
        out_specs=pl.BlockSpec((1, tm, D), lambda b, t: (b, t, 0)),
        out_shape=jax.ShapeDtypeStruct((B, S, D), F32),
        compiler_params=_params(("parallel", "parallel")),
        name="cross_attn",
    )(h, kv, pre_g, wq, wo, post_g)


def _rope_tables(n, dim):
    inv = 1.0 / (ROPE_THETA ** (jnp.arange(0, dim, 2, dtype=F32) / dim))
    ang = jnp.arange(n, dtype=F32)[:, None] * inv[None, :]
    return jnp.cos(ang), jnp.sin(ang)


def _pack_w_in(w):
    L, D, _ = w.shape
    z = lambda n: jnp.zeros((L, D, n), w.dtype)
    o = 0
    cuts = {}
    for name, n in (("cq", MLA_Q_LORA), ("ckv", MLA_KV_LORA), ("kpe", MLA_ROPE_DIM), ("nq", 256), ("kc", 64),
                    ("vc", 64), ("ksl", 64), ("vsl", 64), ("kw", 64), ("vw", 64), ("gn", N_GATES),
                    ("fq", 256), ("fk", 256), ("fv", 256), ("ff", FOX_HEADS), ("s5", S5_WIDTH)):
        cuts[name] = w[:, :, o:o + n]
        o += n
    c = cuts
    return jnp.concatenate([
        c["cq"], z(64), c["ckv"], z(MLA_NOPE_DIM), c["kpe"], z(32), c["nq"], c["kc"], c["vc"], c["ksl"], c["vsl"],
        c["kw"], c["vw"], c["gn"], c["ff"], z(LANES - N_GATES - FOX_HEADS), c["fq"], c["fk"], c["fv"], c["s5"],
    ], axis=-1)


def kernel(x, mem, ffn1_pre_g, ffn1_w_gate, ffn1_w_up, ffn1_w_down, ffn1_post_g, mix_pre_g, mix_w_in, mla_q_norm_g, mla_w_uq, mla_kv_norm_g, mla_w_ukv, nsa_cmp_pos, nsa_phi_w1, nsa_phi_b1, nsa_phi_w2, nsa_gate_b, fox_f_b, s5_lambda_re, s5_lambda_im, s5_log_dt, s5_b_re, s5_b_im, s5_c_re, s5_c_im, s5_d, s5_glu_w, s5_glu_b, mix_group_g, mix_w_out, mix_post_g, xa_pre_g, xa_mem_g, xa_w_q, xa_w_kv, xa_w_o, xa_post_g, ffn2_pre_g, ffn2_w_gate, ffn2_w_up, ffn2_w_down, ffn2_post_g):
    B, S, D = x.shape
    L = ffn1_pre_g.shape[0]
    T = B * S
    G, P, C = S5_GROUPS, S5_STATE, S5_GROUP
    bf = lambda a: a.astype(BF16)

    w_in = bf(_pack_w_in(mix_w_in))
    qg = jnp.pad(mla_q_norm_g, ((0, 0), (0, 256 - MLA_Q_LORA)))
    wuq = mla_w_uq.reshape(L, MLA_Q_LORA, MLA_HEADS, MLA_NOPE_DIM + MLA_ROPE_DIM)
    wuq = jnp.pad(wuq, ((0, 0), (0, 256 - MLA_Q_LORA), (0, 0), (0, MLA_PAD_DIM - MLA_NOPE_DIM - MLA_ROPE_DIM)))
    wuq = bf(wuq.reshape(L, 256, MLA_HEADS * MLA_PAD_DIM))
    wukv = mla_w_ukv.reshape(L, MLA_KV_LORA, MLA_HEADS, MLA_NOPE_DIM + MLA_V_DIM)
    wk = jnp.pad(wukv[..., :MLA_NOPE_DIM], ((0, 0), (0, 0), (0, 0), (0, MLA_PAD_DIM - MLA_NOPE_DIM)))
    wukv = bf(jnp.concatenate([wk.reshape(L, MLA_KV_LORA, -1), wukv[..., MLA_NOPE_DIM:].reshape(L, MLA_KV_LORA, -1)], axis=-1))
    misc_bias = jnp.concatenate([nsa_gate_b, fox_f_b, jnp.zeros((L, LANES - N_GATES - FOX_HEADS), F32)], axis=-1)

    c32, s32 = _rope_tables(S, HEAD_DIM)
    cos64 = jnp.tile(jnp.concatenate([c32, c32], axis=-1), (1, 2))
    sin64 = jnp.tile(jnp.concatenate([-s32, s32], axis=-1), (1, 2))
    c16, s16 = _rope_tables(S, MLA_ROPE_DIM)
    one, zero = jnp.ones((S, 1), F32), jnp.zeros((S, 1), F32)
    cosm = jnp.concatenate([jnp.tile(one, (1, 64)), c16, c16, jnp.tile(one, (1, 32))], axis=-1)
    sinm = jnp.concatenate([jnp.tile(zero, (1, 64)), -s16, s16, jnp.tile(zero, (1, 32))], axis=-1)

    cmp_pos = nsa_cmp_pos.reshape(L * 2, 1, NSA_CMP_LEN * HEAD_DIM)
    phi_w1 = bf(nsa_phi_w1.reshape(L * 2, NSA_CMP_LEN * HEAD_DIM, NSA_CMP_HIDDEN))
    phi_b1 = nsa_phi_b1.reshape(L * 2, 1, NSA_CMP_HIDDEN)
    phi_w2 = bf(nsa_phi_w2.reshape(L * 2, NSA_CMP_HIDDEN, HEAD_DIM))

    n_cmp_pad = S // NSA_CMP_STRIDE
    n_blk = S // NSA_SEL_LEN
    cs = np.arange(n_cmp_pad) * NSA_CMP_STRIDE
    ss = np.arange(LANES) * NSA_SEL_LEN
    ovl_np = ((cs[:, None] < ss[None, :] + NSA_SEL_LEN) & (cs[:, None] + NSA_CMP_LEN > ss[None, :])
              & (np.arange(LANES)[None, :] < n_blk) & (np.arange(n_cmp_pad)[:, None] < n_cmp_pad - 1))
    ovl = jnp.asarray(ovl_np.astype(np.float32), dtype=BF16)
    tq_nsa = 256
    key_blk = np.arange(S) // NSA_SEL_LEN
    expand = jnp.asarray((key_blk[None, :] == np.arange(LANES)[:, None]).astype(np.float32), dtype=BF16)

    dt = jnp.exp(s5_log_dt)[:, :, None]
    lr, li = s5_lambda_re, s5_lambda_im
    mag = jnp.exp(lr * dt)
    a_r, a_i = mag * jnp.cos(li * dt), mag * jnp.sin(li * dt)
    den = lr * lr + li * li
    k_r = ((a_r - 1.0) * lr + a_i * li) / den
    k_i = (a_i * lr - (a_r - 1.0) * li) / den
    flat = lambda a: a.reshape(L, 1, G * P)
    a_r, a_i, k_r, k_i = flat(a_r), flat(a_i), flat(k_r), flat(k_i)
    eye = jnp.eye(G, dtype=F32)
    blk_b = lambda b: jnp.einsum('lgpc,gh->lgchp', b, eye).reshape(L, G * C, G * P)
    blk_c = lambda c: jnp.einsum('lgcp,gh->lgphc', c, eye).reshape(L, G * P, G * C)
    b_re, b_im = blk_b(s5_b_re), blk_b(s5_b_im)
    c_re, c_im = bf(blk_c(s5_c_re)), bf(blk_c(s5_c_im))
    d_skip = s5_d.reshape(L, 1, S5_WIDTH)

    ffn_w = [(bf(ffn1_w_gate), bf(ffn1_w_up), bf(ffn1_w_down)), (bf(ffn2_w_gate), bf(ffn2_w_up), bf(ffn2_w_down))]
    glu_w, w_out = bf(s5_glu_w), bf(mix_w_out)
    wq_x, wkv_x, wo_x = bf(xa_w_q), bf(xa_w_kv), bf(xa_w_o)

    v3 = lambda a: a.reshape(L, 1, a.shape[-1])
    ffn1_pre_g, ffn1_post_g, ffn2_pre_g, ffn2_post_g = v3(ffn1_pre_g), v3(ffn1_post_g), v3(ffn2_pre_g), v3(ffn2_post_g)
    mix_pre_g, mix_post_g, qg, kvg, misc_bias = v3(mix_pre_g), v3(mix_post_g), v3(qg), v3(mla_kv_norm_g), v3(misc_bias)
    s5_glu_b, xa_mem_g, xa_pre_g, xa_post_g = v3(s5_glu_b), v3(xa_mem_g), v3(xa_pre_g), v3(xa_post_g)

    h = x
    for l in range(L):
        h = _ffn(h.reshape(T, D), ffn1_pre_g, *ffn_w[0], ffn1_post_g, l).reshape(B, S, D)

        (mq, mk, mv, nq, nkc, nvc, nksl, nvsl, nkw, nvw, misc, fq, fk, fv, s5u) = _proj(
            h, mix_pre_g, w_in, qg, wuq, kvg, wukv, misc_bias, cos64, sin64, cosm, sinm, l)

        o_mla = _causal_attn(mq, mk, mv)

        cum = _cumsum(misc)[:, :, N_GATES:N_GATES + FOX_HEADS]
        cum_t = jnp.transpose(cum, (0, 2, 1))
        o_fox = _causal_attn(fq, fk, fv, cum_t[..., None], cum_t[:, :, None, :])

        t2 = jnp.stack([nkc, nvc]).reshape(2, B, n_cmp_pad, NSA_CMP_STRIDE * HEAD_DIM)
        cmp = _nsa_compress(t2, cmp_pos, phi_w1, phi_b1, phi_w2, l)
        o_nsa = _nsa_attn(nq, cmp, nksl, nvsl, nkw, nvw, misc, ovl, expand, tq=tq_nsa)

        y_s5 = _s5_scan(jnp.transpose(s5u, (1, 0, 2)), b_re[l], b_im[l], k_r[l], k_i[l], a_r[l], a_i[l],
                        c_re[l], c_im[l], d_skip[l])
        y_s5 = jnp.transpose(y_s5, (1, 0, 2))

        h = _mix_out(h, o_mla, o_nsa, o_fox, y_s5, glu_w, s5_glu_b, mix_group_g, w_out, mix_post_g, l)

        kv = _xa_kv(mem, xa_mem_g, wkv_x, l)
        h = _xa(h, kv, xa_pre_g, wq_x, wo_x, xa_post_g, l)

        h = _ffn(h.reshape(T, D), ffn2_pre_g, *ffn_w[1], ffn2_post_g, l).reshape(B, S, D)
    return h
```

```python
import functools
import math

import numpy as np
import jax
import jax.numpy as jnp
from jax import lax
from jax.experimental import pallas as pl
from jax.experimental.pallas import tpu as pltpu

F32 = jnp.float32
BF16 = jnp.bfloat16

D_MODEL = 1024
HEAD_DIM = 64
ROPE_THETA = 10000.0
NORM_EPS = 1e-6
NEG_INF = -1e30
LOG2E = math.log2(math.e)
MACARON_WEIGHT = 0.5
FFN_DIM = 2816
XA_HEADS = 4
XA_HEAD_DIM = D_MODEL // XA_HEADS

MLA_HEADS = 4
MLA_Q_LORA = 192
MLA_KV_LORA = 128
MLA_NOPE_DIM = 64
MLA_ROPE_DIM = 32
MLA_V_DIM = 64
MLA_PAD_DIM = 128

NSA_HEADS = 4
NSA_CMP_LEN = 32
NSA_CMP_STRIDE = 16
NSA_CMP_HIDDEN = 256
NSA_SEL_LEN = 64
NSA_SEL_BLOCKS = 8
NSA_WINDOW = 256
NSA_FORCE_SCORE = 1e4

FOX_HEADS = 4

S5_WIDTH = 256
S5_GROUP = 16
S5_GROUPS = S5_WIDTH // S5_GROUP
S5_STATE = 64
S5_STATES = S5_GROUPS * S5_STATE

GROUP_WIDTH = 256
LANES = 128
VMEM_LIMIT = 56 * 1024 * 1024

SEG_CQ, SEG_CKV, SEG_KPE, SEG_NQ = 0, 256, 384, 512
SEG_NCMP, SEG_NSLC, SEG_NSWA, SEG_MISC = 768, 896, 1024, 1152
SEG_FQ, SEG_FK, SEG_FV, SEG_S5, IN_PACKED = 1280, 1536, 1792, 2048, 2304
N_GATES = 3 * NSA_HEADS


def _params(sem):
    return pltpu.CompilerParams(dimension_semantics=sem, vmem_limit_bytes=VMEM_LIMIT)


def _rms(x, g, n=None):
    n = x.shape[-1] if n is None else n
    ms = jnp.sum(x * x, axis=-1, keepdims=True) * (1.0 / n)
    return x * lax.rsqrt(ms + NORM_EPS) * g


def _dot(a, b):
    return jnp.dot(a, b, preferred_element_type=F32)


def _dot_nt(a, b):
    return lax.dot_general(a, b, (((1,), (1,)), ((), ())), preferred_element_type=F32)


def _split3(x):
    hi = x.astype(BF16)
    r = x - hi.astype(F32)
    mid = r.astype(BF16)
    lo = (r - mid.astype(F32)).astype(BF16)
    return hi, mid, lo


def _masked_softmax2(s, keep):
    s = s + ((1.0 - keep) * NEG_INF)[None]
    m = jnp.max(s, axis=-1, keepdims=True)
    e = jnp.exp2(s - m) * keep[None]
    return e / jnp.maximum(jnp.sum(e, axis=-1, keepdims=True), 1e-30)


def _ffn_kernel(h_ref, pre_g_ref, wg_ref, wu_ref, wd_ref, post_g_ref, o_ref, *, chunk):
    h = h_ref[...]
    u = _rms(h, pre_g_ref[...]).astype(BF16)
    acc = jnp.zeros(h.shape, F32)
    for c in range(wg_ref.shape[1] // chunk):
        sl = slice(c * chunk, (c + 1) * chunk)
        g = _dot(u, wg_ref[:, sl])
        p = _dot(u, wu_ref[:, sl])
        a = (g * jax.nn.sigmoid(g)) * p
        acc = acc + _dot(a.astype(BF16), wd_ref[sl, :])
    o_ref[...] = h + MACARON_WEIGHT * _rms(acc, post_g_ref[...])


def _ffn(h, pre_g, wg, wu, wd, post_g, layer, tm=512, chunk=256):
    T, D = h.shape
    F = wg.shape[-1]
    wspec = functools.partial(pl.BlockSpec, pipeline_mode=pl.Buffered(1))
    return pl.pallas_call(
        functools.partial(_ffn_kernel, chunk=chunk),
        grid=(T // tm,),
        in_specs=[
            pl.BlockSpec((tm, D), lambda i: (i, 0)),
            pl.BlockSpec((None, 1, D), lambda i: (layer, 0, 0)),
            wspec((None, D, F), lambda i: (layer, 0, 0)),
            wspec((None, D, F), lambda i: (layer, 0, 0)),
            wspec((None, F, D), lambda i: (layer, 0, 0)),
            pl.BlockSpec((None, 1, D), lambda i: (layer, 0, 0)),
        ],
        out_specs=pl.BlockSpec((tm, D), lambda i: (i, 0)),
        out_shape=jax.ShapeDtypeStruct((T, D), F32),
        compiler_params=_params(("parallel",)),
        name="ffn",
    )(h, pre_g, wg, wu, wd, post_g)


def _proj_kernel(h_ref, g_ref, win_ref, qg_ref, wuq_ref, kvg_ref, wukv_ref, bias_ref,
                 cos64_ref, sin64_ref, cosm_ref, sinm_ref,
                 mq_ref, mk_ref, mv_ref, nq_ref, nkc_ref, nvc_ref, nksl_ref, nvsl_ref, nkw_ref, nvw_ref,
                 misc_ref, fq_ref, fk_ref, fv_ref, s5_ref):
    h = h_ref[0]
    tm = h.shape[0]
    u = _rms(h, g_ref[...]).astype(BF16)
    z = _dot(u, win_ref[...])
    lane = lax.broadcasted_iota(jnp.int32, (tm, LANES), 1)

    def rope(x, c, s, half):
        fwd = pltpu.roll(x, LANES - half, 1)
        bwd = pltpu.roll(x, half, 1)
        return x * c + jnp.where(lane % (2 * half) < half, fwd, bwd) * s

    cos64, sin64 = cos64_ref[...], sin64_ref[...]
    cosm, sinm = cosm_ref[...], sinm_ref[...]
    half_m = MLA_ROPE_DIM // 2
    half_n = HEAD_DIM // 2

    cq = _rms(z[:, SEG_CQ:SEG_CKV], qg_ref[...], MLA_Q_LORA).astype(BF16)
    q = _dot(cq, wuq_ref[...])
    ckv = _rms(z[:, SEG_CKV:SEG_KPE], kvg_ref[...]).astype(BF16)
    kv = _dot(ckv, wukv_ref[...])
    kpe = rope(z[:, SEG_KPE:SEG_NQ], cosm, sinm, half_m)
    mla_scale = (MLA_NOPE_DIM + MLA_ROPE_DIM) ** -0.5 * LOG2E
    for hh in range(MLA_HEADS):
        sl = slice(hh * MLA_PAD_DIM, (hh + 1) * MLA_PAD_DIM)
        mq_ref[0, hh] = (rope(q[:, sl], cosm, sinm, half_m) * mla_scale).astype(BF16)
        mk_ref[0, hh] = (kv[:, sl] + kpe).astype(BF16)
        v0 = MLA_HEADS * MLA_PAD_DIM + hh * MLA_V_DIM
        mv_ref[0, hh] = kv[:, v0:v0 + MLA_V_DIM].astype(BF16)

    scale = HEAD_DIM ** -0.5 * LOG2E
    for c in range(NSA_HEADS // 2):
        r = rope(z[:, SEG_NQ + c * LANES:SEG_NQ + (c + 1) * LANES], cos64, sin64, half_n) * scale
        nq_ref[0, 2 * c] = r[:, :HEAD_DIM].astype(BF16)
        nq_ref[0, 2 * c + 1] = r[:, HEAD_DIM:].astype(BF16)
    seg = z[:, SEG_NCMP:SEG_NSLC]
    nkc_ref[0] = rope(seg, cos64, sin64, half_n)[:, :HEAD_DIM]
    nvc_ref[0] = seg[:, HEAD_DIM:]
    seg = z[:, SEG_NSLC:SEG_NSWA]
    nksl_ref[0] = rope(seg, cos64, sin64, half_n)[:, :HEAD_DIM].astype(BF16)
    nvsl_ref[0] = seg[:, HEAD_DIM:].astype(BF16)
    seg = z[:, SEG_NSWA:SEG_MISC]
    nkw_ref[0] = rope(seg, cos64, sin64, half_n)[:, :HEAD_DIM].astype(BF16)
    nvw_ref[0] = seg[:, HEAD_DIM:].astype(BF16)

    x = z[:, SEG_MISC:SEG_FQ] + bias_ref[...]
    log_sig = jnp.minimum(x, 0.0) - jnp.log(1.0 + jnp.exp(-jnp.abs(x)))
    misc_ref[0] = jnp.where(lane < N_GATES, jax.nn.sigmoid(x), log_sig * LOG2E)

    for hh in range(FOX_HEADS):
        sl = slice(hh * HEAD_DIM, (hh + 1) * HEAD_DIM)
        fq_ref[0, hh] = (z[:, SEG_FQ:SEG_FK][:, sl] * scale).astype(BF16)
        fk_ref[0, hh] = z[:, SEG_FK:SEG_FV][:, sl].astype(BF16)
        fv_ref[0, hh] = z[:, SEG_FV:SEG_S5][:, sl].astype(BF16)

    s5_ref[0] = z[:, SEG_S5:IN_PACKED]


def _proj(h, g, win, qg, wuq, kvg, wukv, bias, cos64, sin64, cosm, sinm, layer, tm=512):
    B, S, D = h.shape
    nt = S // tm
    wspec = functools.partial(pl.BlockSpec, pipeline_mode=pl.Buffered(1))

    def w2(shape):
        return wspec((None,) + shape, lambda b, t: (layer, 0, 0))

    def vec(n):
        return pl.BlockSpec((None, 1, n), lambda b, t: (layer, 0, 0))

    def tab():
        return pl.BlockSpec((tm, LANES), lambda b, t: (t, 0))

    def heads(d):
        return pl.BlockSpec((1, 4, tm, d), lambda b, t: (b, 0, t, 0))

    def tok(d):
        return pl.BlockSpec((1, tm, d), lambda b, t: (b, t, 0))

    def hshape(d, dt=BF16):
        return jax.ShapeDtypeStruct((B, 4, S, d), dt)

    def tshape(d, dt):
        return jax.ShapeDtypeStruct((B, S, d), dt)

    return pl.pallas_call(
        _proj_kernel,
        grid=(B, nt),
        in_specs=[
            tok(D), vec(D), w2((D, IN_PACKED)), vec(256), w2((256, 4 * MLA_PAD_DIM)),
            vec(MLA_KV_LORA), w2((MLA_KV_LORA, 4 * MLA_PAD_DIM + 4 * MLA_V_DIM)), vec(LANES),
            tab(), tab(), tab(), tab(),
        ],
        out_specs=[
            heads(MLA_PAD_DIM), heads(MLA_PAD_DIM), heads(MLA_V_DIM),
            heads(HEAD_DIM), tok(HEAD_DIM), tok(HEAD_DIM), tok(HEAD_DIM), tok(HEAD_DIM), tok(HEAD_DIM), tok(HEAD_DIM),
            tok(LANES), heads(HEAD_DIM), heads(HEAD_DIM), heads(HEAD_DIM), tok(S5_WIDTH),
        ],
        out_shape=[
            hshape(MLA_PAD_DIM), hshape(MLA_PAD_DIM), hshape(MLA_V_DIM),
            hshape(HEAD_DIM), tshape(HEAD_DIM, F32), tshape(HEAD_DIM, F32),
            tshape(HEAD_DIM, BF16), tshape(HEAD_DIM, BF16), tshape(HEAD_DIM, BF16), tshape(HEAD_DIM, BF16),
            tshape(LANES, F32), hshape(HEAD_DIM), hshape(HEAD_DIM), hshape(HEAD_DIM), tshape(S5_WIDTH, F32),
        ],
        compiler_params=_params(("parallel", "parallel")),
        name="mix_proj",
    )(h, g, win, qg, wuq, kvg, wukv, bias, cos64, sin64, cosm, sinm)


def _cumsum_kernel(x_ref, o_ref, *, blk):
    S = x_ref.shape[1]
    r = lax.broadcasted_iota(jnp.int32, (blk, blk), 0)
    c = lax.broadcasted_iota(jnp.int32, (blk, blk), 1)
    tri = jnp.where(r >= c, 1.0, 0.0).astype(BF16)
    carry = jnp.zeros((1, x_ref.shape[2]), F32)
    for i in range(S // blk):
        hi, mid, lo = _split3(x_ref[0, i * blk:(i + 1) * blk, :])
        cs = _dot(tri, hi) + _dot(tri, mid) + _dot(tri, lo) + carry
        o_ref[0, i * blk:(i + 1) * blk, :] = cs
        carry = cs[blk - 1:blk, :]


def _cumsum(x, blk=256):
    B, S, W = x.shape
    return pl.pallas_call(
        functools.partial(_cumsum_kernel, blk=blk),
        grid=(B,),
        in_specs=[pl.BlockSpec((1, S, W), lambda b: (b, 0, 0))],
        out_specs=pl.BlockSpec((1, S, W), lambda b: (b, 0, 0)),
        out_shape=jax.ShapeDtypeStruct((B, S, W), F32),
        compiler_params=_params(("parallel",)),
        name="fox_cumsum",
    )(x)


def _tri_masks(tq):
    r = lax.broadcasted_iota(jnp.int32, (tq, tq), 0)
    c = lax.broadcasted_iota(jnp.int32, (tq, tq), 1)
    keep = jnp.where(r >= c, 1.0, 0.0)
    return keep, (1.0 - keep) * NEG_INF


def _causal_attn_kernel(*refs, tq, has_bias):
    if has_bias:
        q_ref, k_ref, v_ref, cq_ref, ck_ref, o_ref = refs
    else:
        q_ref, k_ref, v_ref, o_ref = refs
    S = q_ref.shape[2]
    keep, bias_tri = _tri_masks(tq)
    for i in range(S // tq):
        lo, hi = i * tq, (i + 1) * tq
        outs = []
        for hh in range(q_ref.shape[1]):
            q = q_ref[0, hh, lo:hi, :]
            s_d = _dot_nt(q, k_ref[0, hh, lo:hi, :]) + bias_tri
            if has_bias:
                cq = cq_ref[0, hh, lo:hi, :]
                s_d = s_d + (cq - ck_ref[0, hh, :, lo:hi])
            m = jnp.max(s_d, axis=-1, keepdims=True)
            if i > 0:
                s_o = _dot_nt(q, k_ref[0, hh, :lo, :])
                if has_bias:
                    s_o = s_o + (cq - ck_ref[0, hh, :, :lo])
                m = jnp.maximum(m, jnp.max(s_o, axis=-1, keepdims=True))
            p_d = jnp.exp2(s_d - m) * keep
            l = jnp.sum(p_d, axis=-1, keepdims=True)
            acc = _dot(p_d.astype(BF16), v_ref[0, hh, lo:hi, :])
            if i > 0:
                p_o = jnp.exp2(s_o - m)
                l = l + jnp.sum(p_o, axis=-1, keepdims=True)
                acc = acc + _dot(p_o.astype(BF16), v_ref[0, hh, :lo, :])
            outs.append(acc / jnp.maximum(l, 1e-30))
        o_ref[0, lo:hi, :] = jnp.concatenate(outs, axis=-1)


def _causal_attn(q, k, v, cum_col=None, cum_row=None, tq=256, hpb=2):
    B, H, S, dk = q.shape
    dv = v.shape[-1]
    has_bias = cum_col is not None

    def heads(w):
        return pl.BlockSpec((1, hpb, S, w), lambda b, h: (b, h, 0, 0))

    in_specs = [heads(dk), heads(dk), heads(dv)]
    args = [q, k, v]
    if has_bias:
        in_specs += [heads(1), pl.BlockSpec((1, hpb, 1, S), lambda b, h: (b, h, 0, 0))]
        args += [cum_col, cum_row]
    return pl.pallas_call(
        functools.partial(_causal_attn_kernel, tq=tq, has_bias=has_bias),
        grid=(B, H // hpb),
        in_specs=in_specs,
        out_specs=pl.BlockSpec((1, S, hpb * dv), lambda b, h: (b, 0, h)),
        out_shape=jax.ShapeDtypeStruct((B, S, H * dv), F32),
        compiler_params=_params(("parallel", "parallel")),
        name="fox_attn" if has_bias else "mla_attn",
    )(*args)


def _nsa_compress_kernel(t_ref, pos_ref, w1_ref, b1_ref, w2_ref, o_ref):
    t = t_ref[0, 0]
    half = t.shape[1]
    a = _dot((t + pos_ref[0, :, :half]).astype(BF16), w1_ref[0, :half, :])
    b = _dot((t + pos_ref[0, :, half:]).astype(BF16), w1_ref[0, half:, :])
    nrow = t.shape[0]
    hid = jax.nn.gelu(a + pltpu.roll(b, nrow - 1, 0) + b1_ref[0])
    o_ref[0, 0] = _dot(hid.astype(BF16), w2_ref[0]).astype(BF16)


def _nsa_compress(t2, pos, w1, b1, w2, layer):
    _, B, R, W = t2.shape
    return pl.pallas_call(
        _nsa_compress_kernel,
        grid=(2, B),
        in_specs=[
            pl.BlockSpec((1, 1, R, W), lambda j, b: (j, b, 0, 0)),
            pl.BlockSpec((1, 1, 2 * W), lambda j, b: (2 * layer + j, 0, 0)),
            pl.BlockSpec((1, 2 * W, NSA_CMP_HIDDEN), lambda j, b: (2 * layer + j, 0, 0)),
            pl.BlockSpec((1, 1, NSA_CMP_HIDDEN), lambda j, b: (2 * layer + j, 0, 0)),
            pl.BlockSpec((1, NSA_CMP_HIDDEN, HEAD_DIM), lambda j, b: (2 * layer + j, 0, 0)),
        ],
        out_specs=pl.BlockSpec((1, 1, R, HEAD_DIM), lambda j, b: (j, b, 0, 0)),
        out_shape=jax.ShapeDtypeStruct((2, B, R, HEAD_DIM), BF16),
        compiler_params=_params(("parallel", "parallel")),
        name="nsa_compress",
    )(t2, pos, w1, b1, w2)


def _nsa_attn_kernel(q_ref, cmp_ref, ksl_ref, vsl_ref, kw_ref, vw_ref, gate_ref, ovl_ref, exp_ref, o_ref, slc_ref, *, tq):
    H = NSA_HEADS
    qi = pl.program_id(1)
    lo = qi * tq
    q4 = q_ref[0].reshape(H * tq, HEAD_DIM)
    pos_q = lo + lax.broadcasted_iota(jnp.int32, (tq, 1), 0)
    lane = lax.broadcasted_iota(jnp.int32, (1, LANES), 1)

    kc, vc = cmp_ref[0, 0], cmp_ref[1, 0]
    nc = kc.shape[0]
    cmp_end = lax.broadcasted_iota(jnp.int32, (1, nc), 1) * NSA_CMP_STRIDE + (NSA_CMP_LEN - 1)
    keep_c = jnp.where(pos_q >= cmp_end, 1.0, 0.0)
    p_c = _masked_softmax2(_dot_nt(q4, kc).reshape(H, tq, nc), keep_c)
    o_cmp = _dot(p_c.reshape(H * tq, nc).astype(BF16), vc).reshape(H, tq, HEAD_DIM)

    p_sum = p_c[0] + p_c[1] + p_c[2] + p_c[3]
    hi, mid, lo3 = _split3(p_sum)
    ovl = ovl_ref[...]
    imp = _dot(hi, ovl) + _dot(mid, ovl) + _dot(lo3, ovl)
    cur = pos_q // NSA_SEL_LEN
    forced = (lane == 0) | (lane == cur) | (lane == cur - 1)
    score = jnp.where(forced, NSA_FORCE_SCORE, imp)
    score = jnp.where(lane <= cur, score, NEG_INF)
    n_blk = exp_ref.shape[1] // NSA_SEL_LEN
    score_t = score.T[:n_blk]
    blk = lax.broadcasted_iota(jnp.int32, (n_blk, 1), 0)
    cnt = jnp.zeros(score_t.shape, F32)
    for kk in range(n_blk):
        row = score_t[kk:kk + 1, :]
        beats = (row > score_t) | ((row == score_t) & (blk > kk))
        cnt = cnt + jnp.where(beats, 1.0, 0.0)
    sel_t = jnp.where(cnt < min(NSA_SEL_BLOCKS, n_blk), 1.0, 0.0)
    sel = jnp.concatenate([sel_t, jnp.zeros((LANES - n_blk, tq), F32)], axis=0).T.astype(BF16)

    keep_tri, _ = _tri_masks(tq)

    def slc_variant(i):
        lo_k, hi_k = i * tq, (i + 1) * tq
        keep = _dot(sel, exp_ref[:, :hi_k])
        m = jnp.full((H, tq, 1), NEG_INF, F32)
        l = jnp.zeros((H, tq, 1), F32)
        acc = jnp.zeros((H, tq, HEAD_DIM), F32)
        for c in range(i + 1):
            ks = slice(c * tq, (c + 1) * tq)
            keep_c = keep[:, ks] * keep_tri if c == i else keep[:, ks]
            s = _dot_nt(q4, ksl_ref[0, ks, :]).reshape(H, tq, tq) + ((1.0 - keep_c) * NEG_INF)[None]
            m_new = jnp.maximum(m, jnp.max(s, axis=-1, keepdims=True))
            alpha = jnp.exp2(m - m_new)
            p = jnp.exp2(s - m_new) * keep_c[None]
            l = alpha * l + jnp.sum(p, axis=-1, keepdims=True)
            pv = _dot(p.reshape(H * tq, tq).astype(BF16), vsl_ref[0, ks, :]).reshape(H, tq, HEAD_DIM)
            acc = alpha * acc + pv
            m = m_new
        slc_ref[...] = acc / jnp.maximum(l, 1e-30)

    for i in range(ksl_ref.shape[1] // tq):
        pl.when(qi == i)(functools.partial(slc_variant, i))
    o_slc = slc_ref[...]

    nw = NSA_WINDOW + tq
    start = pl.multiple_of(jnp.maximum(lo - NSA_WINDOW, 0), tq)
    kw = kw_ref[0, pl.ds(start, nw), :]
    vw = vw_ref[0, pl.ds(start, nw), :]
    kpos = start + lax.broadcasted_iota(jnp.int32, (1, nw), 1)
    keep_w = jnp.where((kpos <= pos_q) & (kpos > pos_q - NSA_WINDOW), 1.0, 0.0)
    p_w = _masked_softmax2(_dot_nt(q4, kw).reshape(H, tq, nw), keep_w)
    o_swa = _dot(p_w.reshape(H * tq, nw).astype(BF16), vw).reshape(H, tq, HEAD_DIM)

    g = gate_ref[0]
    o_ref[0] = jnp.concatenate(
        [g[:, hh:hh + 1] * o_cmp[hh] + g[:, H + hh:H + hh + 1] * o_slc[hh] + g[:, 2 * H + hh:2 * H + hh + 1] * o_swa[hh]
         for hh in range(H)], axis=-1)


def _nsa_attn(q, cmp, ksl, vsl, kw, vw, gates, ovl, expand, tq=256):
    B, H, S, d = q.shape
    R = cmp.shape[2]

    def full(w):
        return pl.BlockSpec((1, S, w), lambda b, i: (b, 0, 0))

    return pl.pallas_call(
        functools.partial(_nsa_attn_kernel, tq=tq),
        grid=(B, S // tq),
        in_specs=[
            pl.BlockSpec((1, H, tq, d), lambda b, i: (b, 0, i, 0)),
            pl.BlockSpec((2, 1, R, d), lambda b, i: (0, b, 0, 0)),
            full(d), full(d), full(d), full(d),
            pl.BlockSpec((1, tq, LANES), lambda b, i: (b, i, 0)),
            pl.BlockSpec((R, LANES), lambda b, i: (0, 0)),
            pl.BlockSpec((LANES, S), lambda b, i: (0, 0)),
        ],
        out_specs=pl.BlockSpec((1, tq, H * d), lambda b, i: (b, i, 0)),
        out_shape=jax.ShapeDtypeStruct((B, S, H * d), F32),
        scratch_shapes=[pltpu.VMEM((H, tq, d), F32)],
        compiler_params=_params(("parallel", "arbitrary")),
        name="nsa_attn",
    )(q, cmp, ksl, vsl, kw, vw, gates, ovl, expand)


def _s5_kernel(u_ref, bre_ref, bim_ref, kr_ref, ki_ref, ar_ref, ai_ref, cre_ref, cim_ref, d_ref, y_ref,
               xr_ref, xi_ref, sr_ref, si_ref, *, tc, nb, lane_blk):
    @pl.when(pl.program_id(0) == 0)
    def _():
        sr_ref[...] = jnp.zeros(sr_ref.shape, F32)
        si_ref[...] = jnp.zeros(si_ref.shape, F32)

    u = u_ref[...].reshape(tc * nb, S5_WIDTH)
    ub = u.astype(BF16)
    kr, ki = kr_ref[...], ki_ref[...]
    bre, bim = bre_ref[...], bim_ref[...]
    xr_ref[...] = _dot(ub, (bre * kr - bim * ki).astype(BF16))
    xi_ref[...] = _dot(ub, (bre * ki + bim * kr).astype(BF16))

    for c in range(S5_STATES // lane_blk):
        sl = slice(c * lane_blk, (c + 1) * lane_blk)
        ar = jnp.broadcast_to(ar_ref[:, sl], (nb, lane_blk))
        ai = jnp.broadcast_to(ai_ref[:, sl], (nb, lane_blk))

        def step(t, carry):
            sr, si = carry
            row = pl.ds(pl.multiple_of(t * nb, nb), nb)
            nr = ar * sr - ai * si + xr_ref[row, sl]
            ni = ar * si + ai * sr + xi_ref[row, sl]
            xr_ref[row, sl] = nr
            xi_ref[row, sl] = ni
            return nr, ni

        sr, si = lax.fori_loop(0, tc, step, (sr_ref[:, sl], si_ref[:, sl]), unroll=4)
        sr_ref[:, sl] = sr
        si_ref[:, sl] = si

    y = (_dot(xr_ref[...].astype(BF16), cre_ref[...]) - _dot(xi_ref[...].astype(BF16), cim_ref[...])
         + d_ref[...] * u)
    y_ref[...] = y.reshape(tc, nb, S5_WIDTH)


def _s5_scan(u_t, bre, bim, kr, ki, ar, ai, cre, cim, d, tc=64, lane_blk=256):
    S, B, W = u_t.shape
    const = lambda shape: pl.BlockSpec(shape, lambda i: (0, 0))
    return pl.pallas_call(
        functools.partial(_s5_kernel, tc=tc, nb=B, lane_blk=lane_blk),
        grid=(S // tc,),
        in_specs=[
            pl.BlockSpec((tc, B, W), lambda i: (i, 0, 0)),
            const((W, S5_STATES)), const((W, S5_STATES)),
            const((1, S5_STATES)), const((1, S5_STATES)), const((1, S5_STATES)), const((1, S5_STATES)),
            const((S5_STATES, W)), const((S5_STATES, W)), const((1, W)),
        ],
        out_specs=pl.BlockSpec((tc, B, W), lambda i: (i, 0, 0)),
        out_shape=jax.ShapeDtypeStruct((S, B, W), F32),
        scratch_shapes=[
            pltpu.VMEM((tc * B, S5_STATES), F32), pltpu.VMEM((tc * B, S5_STATES), F32),
            pltpu.VMEM((B, S5_STATES), F32), pltpu.VMEM((B, S5_STATES), F32),
        ],
        compiler_params=_params(("arbitrary",)),
        name="s5_scan",
    )(u_t, bre, bim, kr, ki, ar, ai, cre, cim, d)


def _mix_out_kernel(h_ref, om_ref, on_ref, of_ref, ys_ref, gw_ref, gb_ref, gg_ref, wo_ref, pg_ref, o_ref):
    gg = gg_ref[0]
    y = jax.nn.gelu(ys_ref[0])
    y = y * jax.nn.sigmoid(_dot(y.astype(BF16), gw_ref[0]) + gb_ref[...])
    groups = (om_ref[0], on_ref[0], of_ref[0], y)
    cat = jnp.concatenate([_rms(p, gg[i:i + 1, :]).astype(BF16) for i, p in enumerate(groups)], axis=-1)
    o_ref[0] = h_ref[0] + _rms(_dot(cat, wo_ref[0]), pg_ref[...])


def _mix_out(h, o_mla, o_nsa, o_fox, y_s5, glu_w, glu_b, group_g, w_out, post_g, layer, tm=512):
    B, S, D = h.shape

    def group():
        return pl.BlockSpec((1, tm, GROUP_WIDTH), lambda b, t: (b, t, 0))

    return pl.pallas_call(
        _mix_out_kernel,
        grid=(B, S // tm),
        in_specs=[
            pl.BlockSpec((1, tm, D), lambda b, t: (b, t, 0)),
            group(), group(), group(), group(),
            pl.BlockSpec((1, S5_WIDTH, S5_WIDTH), lambda b, t: (layer, 0, 0)),
            pl.BlockSpec((None, 1, S5_WIDTH), lambda b, t: (layer, 0, 0)),
            pl.BlockSpec((1, 4, GROUP_WIDTH), lambda b, t: (layer, 0, 0)),
            pl.BlockSpec((1, D, D), lambda b, t: (layer, 0, 0)),
            pl.BlockSpec((None, 1, D), lambda b, t: (layer, 0, 0)),
        ],
        out_specs=pl.BlockSpec((1, tm, D), lambda b, t: (b, t, 0)),
        out_shape=jax.ShapeDtypeStruct((B, S, D), F32),
        compiler_params=_params(("parallel", "parallel")),
        name="mix_out",
    )(h, o_mla, o_nsa, o_fox, y_s5, glu_w, glu_b, group_g, w_out, post_g)


def _xa_kv_kernel(mem_ref, g_ref, wkv_ref, kv_ref):
    m = _rms(mem_ref[0], g_ref[...]).astype(BF16)
    kv_ref[0] = _dot(m, wkv_ref[0]).astype(BF16)


def _xa_kv(mem, g, wkv, layer):
    B, M, D = mem.shape
    return pl.pallas_call(
        _xa_kv_kernel,
        grid=(B,),
        in_specs=[
            pl.BlockSpec((1, M, D), lambda b: (b, 0, 0)),
            pl.BlockSpec((None, 1, D), lambda b: (layer, 0, 0)),
            pl.BlockSpec((1, D, 2 * D), lambda b: (layer, 0, 0)),
        ],
        out_specs=pl.BlockSpec((1, M, 2 * D), lambda b: (b, 0, 0)),
        out_shape=jax.ShapeDtypeStruct((B, M, 2 * D), BF16),
        compiler_params=_params(("parallel",)),
        name="xa_kv",
    )(mem, g, wkv)


def _xa_kernel(h_ref, kv_ref, pre_g_ref, wq_ref, wo_ref, post_g_ref, o_ref):
    h = h_ref[0]
    u = _rms(h, pre_g_ref[...]).astype(BF16)
    q = _dot(u, wq_ref[0]) * (XA_HEAD_DIM ** -0.5)
    acc = jnp.zeros(h.shape, F32)
    for hh in range(XA_HEADS):
        sl = slice(hh * XA_HEAD_DIM, (hh + 1) * XA_HEAD_DIM)
        k = kv_ref[0, :, sl]
        v = kv_ref[0, :, D_MODEL + hh * XA_HEAD_DIM:D_MODEL + (hh + 1) * XA_HEAD_DIM]
        s = _dot_nt(q[:, sl].astype(BF16), k)
        m = jnp.max(s, axis=-1, keepdims=True)
        e = jnp.exp(s - m)
        p = e / jnp.sum(e, axis=-1, keepdims=True)
        o = _dot(p.astype(BF16), v)
        acc = acc + _dot(o.astype(BF16), wo_ref[0, sl, :])
    o_ref[0] = h + _rms(acc, post_g_ref[...])


def _xa(h, kv, pre_g, wq, wo, post_g, layer, tm=512):
    B, S, D = h.shape
    M = kv.shape[1]
    return pl.pallas_call(
        _xa_kernel,
        grid=(B, S // tm),
        in_specs=[
            pl.BlockSpec((1, tm, D), lambda b, t: (b, t, 0)),
            pl.BlockSpec((1, M, 2 * D), lambda b, t: (b, 0, 0)),
            pl.BlockSpec((None, 1, D), lambda b, t: (layer, 0, 0)),
            pl.BlockSpec((1, D, D), lambda b, t: (layer, 0, 0)),
            pl.BlockSpec((1, D, D), lambda b, t: (layer, 0, 0)),
            pl.BlockSpec((None, 1, D), lambda b, t: (layer, 0, 0)),
        ],
        out_specs=pl.BlockSpec((1, tm, D), lambda b, t: (b, t, 0)),
        out_shape=jax.ShapeDtypeStruct((B, S, D), F32),
        compiler_params=_params(("parallel", "parallel")),
        name="cross_attn",
    )(h, kv, pre_g, wq, wo, post_g)


def _rope_tables(n, dim):
    inv = 1.0 / (ROPE_THETA ** (jnp.arange(0, dim, 2, dtype=F32) / dim))
    ang = jnp.arange(n, dtype=F32)[:, None] * inv[None, :]
    return jnp.cos(ang), jnp.sin(ang)


def _pack_w_in(w):
    L, D, _ = w.shape
    z = lambda n: jnp.zeros((L, D, n), w.dtype)
    o = 0
    cuts = {}
    for name, n in (("cq", MLA_Q_LORA), ("ckv", MLA_KV_LORA), ("kpe", MLA_ROPE_DIM), ("nq", 256), ("kc", 64),
                    ("vc", 64), ("ksl", 64), ("vsl", 64), ("kw", 64), ("vw", 64), ("gn", N_GATES),
                    ("fq", 256), ("fk", 256), ("fv", 256), ("ff", FOX_HEADS), ("s5", S5_WIDTH)):
        cuts[name] = w[:, :, o:o + n]
        o += n
    c = cuts
    return jnp.concatenate([
        c["cq"], z(64), c["ckv"], z(MLA_NOPE_DIM), c["kpe"], z(32), c["nq"], c["kc"], c["vc"], c["ksl"], c["vsl"],
        c["kw"], c["vw"], c["gn"], c["ff"], z(LANES - N_GATES - FOX_HEADS), c["fq"], c["fk"], c["fv"], c["s5"],
    ], axis=-1)


def kernel(x, mem, ffn1_pre_g, ffn1_w_gate, ffn1_w_up, ffn1_w_down, ffn1_post_g, mix_pre_g, mix_w_in, mla_q_norm_g, mla_w_uq, mla_kv_norm_g, mla_w_ukv, nsa_cmp_pos, nsa_phi_w1, nsa_phi_b1, nsa_phi_w2, nsa_gate_b, fox_f_b, s5_lambda_re, s5_lambda_im, s5_log_dt, s5_b_re, s5_b_im, s5_c_re, s5_c_im, s5_d, s5_glu_w, s5_glu_b, mix_group_g, mix_w_out, mix_post_g, xa_pre_g, xa_mem_g, xa_w_q, xa_w_kv, xa_w_o, xa_post_g, ffn2_pre_g, ffn2_w_gate, ffn2_w_up, ffn2_w_down, ffn2_post_g):
    B, S, D = x.shape
    L = ffn1_pre_g.shape[0]
    T = B * S
    G, P, C = S5_GROUPS, S5_STATE, S5_GROUP
    bf = lambda a: a.astype(BF16)

    w_in = bf(_pack_w_in(mix_w_in))
    qg = jnp.pad(mla_q_norm_g, ((0, 0), (0, 256 - MLA_Q_LORA)))
    wuq = mla_w_uq.reshape(L, MLA_Q_LORA, MLA_HEADS, MLA_NOPE_DIM + MLA_ROPE_DIM)
    wuq = jnp.pad(wuq, ((0, 0), (0, 256 - MLA_Q_LORA), (0, 0), (0, MLA_PAD_DIM - MLA_NOPE_DIM - MLA_ROPE_DIM)))
    wuq = bf(wuq.reshape(L, 256, MLA_HEADS * MLA_PAD_DIM))
    wukv = mla_w_ukv.reshape(L, MLA_KV_LORA, MLA_HEADS, MLA_NOPE_DIM + MLA_V_DIM)
    wk = jnp.pad(wukv[..., :MLA_NOPE_DIM], ((0, 0), (0, 0), (0, 0), (0, MLA_PAD_DIM - MLA_NOPE_DIM)))
    wukv = bf(jnp.concatenate([wk.reshape(L, MLA_KV_LORA, -1), wukv[..., MLA_NOPE_DIM:].reshape(L, MLA_KV_LORA, -1)], axis=-1))
    misc_bias = jnp.concatenate([nsa_gate_b, fox_f_b, jnp.zeros((L, LANES - N_GATES - FOX_HEADS), F32)], axis=-1)

    c32, s32 = _rope_tables(S, HEAD_DIM)
    cos64 = jnp.tile(jnp.concatenate([c32, c32], axis=-1), (1, 2))
    sin64 = jnp.tile(jnp.concatenate([-s32, s32], axis=-1), (1, 2))
    c16, s16 = _rope_tables(S, MLA_ROPE_DIM)
    one, zero = jnp.ones((S, 1), F32), jnp.zeros((S, 1), F32)
    cosm = jnp.concatenate([jnp.tile(one, (1, 64)), c16, c16, jnp.tile(one, (1, 32))], axis=-1)
    sinm = jnp.concatenate([jnp.tile(zero, (1, 64)), -s16, s16, jnp.tile(zero, (1, 32))], axis=-1)

    cmp_pos = nsa_cmp_pos.reshape(L * 2, 1, NSA_CMP_LEN * HEAD_DIM)
    phi_w1 = bf(nsa_phi_w1.reshape(L * 2, NSA_CMP_LEN * HEAD_DIM, NSA_CMP_HIDDEN))
    phi_b1 = nsa_phi_b1.reshape(L * 2, 1, NSA_CMP_HIDDEN)
    phi_w2 = bf(nsa_phi_w2.reshape(L * 2, NSA_CMP_HIDDEN, HEAD_DIM))

    n_cmp_pad = S // NSA_CMP_STRIDE
    n_blk = S // NSA_SEL_LEN
    cs = np.arange(n_cmp_pad) * NSA_CMP_STRIDE
    ss = np.arange(LANES) * NSA_SEL_LEN
    ovl_np = ((cs[:, None] < ss[None, :] + NSA_SEL_LEN) & (cs[:, None] + NSA_CMP_LEN > ss[None, :])
              & (np.arange(LANES)[None, :] < n_blk) & (np.arange(n_cmp_pad)[:, None] < n_cmp_pad - 1))
    ovl = jnp.asarray(ovl_np.astype(np.float32), dtype=BF16)
    tq_nsa = 256
    key_blk = np.arange(S) // NSA_SEL_LEN
    expand = jnp.asarray((key_blk[None, :] == np.arange(LANES)[:, None]).astype(np.float32), dtype=BF16)

    dt = jnp.exp(s5_log_dt)[:, :, None]
    lr, li = s5_lambda_re, s5_lambda_im
    mag = jnp.exp(lr * dt)
    a_r, a_i = mag * jnp.cos(li * dt), mag * jnp.sin(li * dt)
    den = lr * lr + li * li
    k_r = ((a_r - 1.0) * lr + a_i * li) / den
    k_i = (a_i * lr - (a_r - 1.0) * li) / den
    flat = lambda a: a.reshape(L, 1, G * P)
    a_r, a_i, k_r, k_i = flat(a_r), flat(a_i), flat(k_r), flat(k_i)
    eye = jnp.eye(G, dtype=F32)
    blk_b = lambda b: jnp.einsum('lgpc,gh->lgchp', b, eye).reshape(L, G * C, G * P)
    blk_c = lambda c: jnp.einsum('lgcp,gh->lgphc', c, eye).reshape(L, G * P, G * C)
    b_re, b_im = blk_b(s5_b_re), blk_b(s5_b_im)
    c_re, c_im = bf(blk_c(s5_c_re)), bf(blk_c(s5_c_im))
    d_skip = s5_d.reshape(L, 1, S5_WIDTH)

    ffn_w = [(bf(ffn1_w_gate), bf(ffn1_w_up), bf(ffn1_w_down)), (bf(ffn2_w_gate), bf(ffn2_w_up), bf(ffn2_w_down))]
    glu_w, w_out = bf(s5_glu_w), bf(mix_w_out)
    wq_x, wkv_x, wo_x = bf(xa_w_q), bf(xa_w_kv), bf(xa_w_o)

    v3 = lambda a: a.reshape(L, 1, a.shape[-1])
    ffn1_pre_g, ffn1_post_g, ffn2_pre_g, ffn2_post_g = v3(ffn1_pre_g), v3(ffn1_post_g), v3(ffn2_pre_g), v3(ffn2_post_g)
    mix_pre_g, mix_post_g, qg, kvg, misc_bias = v3(mix_pre_g), v3(mix_post_g), v3(qg), v3(mla_kv_norm_g), v3(misc_bias)
    s5_glu_b, xa_mem_g, xa_pre_g, xa_post_g = v3(s5_glu_b), v3(xa_mem_g), v3(xa_pre_g), v3(xa_post_g)

    h = x
    for l in range(L):
        h = _ffn(h.reshape(T, D), ffn1_pre_g, *ffn_w[0], ffn1_post_g, l).reshape(B, S, D)

        (mq, mk, mv, nq, nkc, nvc, nksl, nvsl, nkw, nvw, misc, fq, fk, fv, s5u) = _proj(
            h, mix_pre_g, w_in, qg, wuq, kvg, wukv, misc_bias, cos64, sin64, cosm, sinm, l)

        o_mla = _causal_attn(mq, mk, mv)

        cum = _cumsum(misc)[:, :, N_GATES:N_GATES + FOX_HEADS]
        cum_t = jnp.transpose(cum, (0, 2, 1))
        o_fox = _causal_attn(fq, fk, fv, cum_t[..., None], cum_t[:, :, None, :])

        t2 = jnp.stack([nkc, nvc]).reshape(2, B, n_cmp_pad, NSA_CMP_STRIDE * HEAD_DIM)
        cmp = _nsa_compress(t2, cmp_pos, phi_w1, phi_b1, phi_w2, l)
        o_nsa = _nsa_attn(nq, cmp, nksl, nvsl, nkw, nvw, misc, ovl, expand, tq=tq_nsa)

        y_s5 = _s5_scan(jnp.transpose(s5u, (1, 0, 2)), b_re[l], b_im[l], k_r[l], k_i[l], a_r[l], a_i[l],
                        c_re[l], c_im[l], d_skip[l])
        y_s5 = jnp.transpose(y_s5, (1, 0, 2))

        h = _mix_out(h, o_mla, o_nsa, o_fox, y_s5, glu_w, s5_glu_b, mix_group_g, w_out, mix_post_g, l)

        kv = _xa_kv(mem, xa_mem_g, wkv_x, l)
        h = _xa(h, kv, xa_pre_g, wq_x, wo_x, xa_post_g, l)

        h = _ffn(h.reshape(T, D), ffn2_pre_g, *ffn_w[1], ffn2_post_g, l).reshape(B, S, D)
    return h
```

```python
import functools
import math

import numpy as np
import jax
import jax.numpy as jnp
from jax import lax
from jax.experimental import pallas as pl
from jax.experimental.pallas import tpu as pltpu

F32 = jnp.float32
BF16 = jnp.bfloat16

D_MODEL = 1024
HEAD_DIM = 64
ROPE_THETA = 10000.0
NORM_EPS = 1e-6
NEG_INF = -1e30
LOG2E = math.log2(math.e)
MACARON_WEIGHT = 0.5
FFN_DIM = 2816
XA_HEADS = 4
XA_HEAD_DIM = D_MODEL // XA_HEADS

MLA_HEADS = 4
MLA_Q_LORA = 192
MLA_KV_LORA = 128
MLA_NOPE_DIM = 64
MLA_ROPE_DIM = 32
MLA_V_DIM = 64
MLA_PAD_DIM = 128

NSA_HEADS = 4
NSA_CMP_LEN = 32
NSA_CMP_STRIDE = 16
NSA_CMP_HIDDEN = 256
NSA_SEL_LEN = 64
NSA_SEL_BLOCKS = 8
NSA_WINDOW = 256
NSA_FORCE_SCORE = 1e4

FOX_HEADS = 4

S5_WIDTH = 256
S5_GROUP = 16
S5_GROUPS = S5_WIDTH // S5_GROUP
S5_STATE = 64
S5_STATES = S5_GROUPS * S5_STATE

GROUP_WIDTH = 256
LANES = 128
VMEM_LIMIT = 56 * 1024 * 1024

SEG_CQ, SEG_CKV, SEG_KPE, SEG_NQ = 0, 256, 384, 512
SEG_NCMP, SEG_NSLC, SEG_NSWA, SEG_MISC = 768, 896, 1024, 1152
SEG_FQ, SEG_FK, SEG_FV, SEG_S5, IN_PACKED = 1280, 1536, 1792, 2048, 2304
N_GATES = 3 * NSA_HEADS


def _params(sem):
    return pltpu.CompilerParams(dimension_semantics=sem, vmem_limit_bytes=VMEM_LIMIT)


def _rms(x, g, n=None):
    n = x.shape[-1] if n is None else n
    ms = jnp.sum(x * x, axis=-1, keepdims=True) * (1.0 / n)
    return x * lax.rsqrt(ms + NORM_EPS) * g


def _dot(a, b):
    return jnp.dot(a, b, preferred_element_type=F32)


def _dot_nt(a, b):
    return lax.dot_general(a, b, (((1,), (1,)), ((), ())), preferred_element_type=F32)


def _split3(x):
    hi = x.astype(BF16)
    r = x - hi.astype(F32)
    mid = r.astype(BF16)
    lo = (r - mid.astype(F32)).astype(BF16)
    return hi, mid, lo


def _masked_softmax2(s, keep):
    s = s + ((1.0 - keep) * NEG_INF)[None]
    m = jnp.max(s, axis=-1, keepdims=True)
    e = jnp.exp2(s - m) * keep[None]
    return e / jnp.maximum(jnp.sum(e, axis=-1, keepdims=True), 1e-30)


def _ffn_kernel(h_ref, pre_g_ref, wg_ref, wu_ref, wd_ref, post_g_ref, o_ref, *, chunk):
    h = h_ref[...]
    u = _rms(h, pre_g_ref[...]).astype(BF16)
    acc = jnp.zeros(h.shape, F32)
    for c in range(wg_ref.shape[1] // chunk):
        sl = slice(c * chunk, (c + 1) * chunk)
        g = _dot(u, wg_ref[:, sl])
        p = _dot(u, wu_ref[:, sl])
        a = (g * jax.nn.sigmoid(g)) * p
        acc = acc + _dot(a.astype(BF16), wd_ref[sl, :])
    o_ref[...] = h + MACARON_WEIGHT * _rms(acc, post_g_ref[...])


def _ffn(h, pre_g, wg, wu, wd, post_g, layer, tm=512, chunk=256):
    T, D = h.shape
    F = wg.shape[-1]
    wspec = functools.partial(pl.BlockSpec, pipeline_mode=pl.Buffered(1))
    return pl.pallas_call(
        functools.partial(_ffn_kernel, chunk=chunk),
        grid=(T // tm,),
        in_specs=[
            pl.BlockSpec((tm, D), lambda i: (i, 0)),
            pl.BlockSpec((None, 1, D), lambda i: (layer, 0, 0)),
            wspec((None, D, F), lambda i: (layer, 0, 0)),
            wspec((None, D, F), lambda i: (layer, 0, 0)),
            wspec((None, F, D), lambda i: (layer, 0, 0)),
            pl.BlockSpec((None, 1, D), lambda i: (layer, 0, 0)),
        ],
        out_specs=pl.BlockSpec((tm, D), lambda i: (i, 0)),
        out_shape=jax.ShapeDtypeStruct((T, D), F32),
        compiler_params=_params(("parallel",)),
        name="ffn",
    )(h, pre_g, wg, wu, wd, post_g)


def _proj_kernel(h_ref, g_ref, win_ref, qg_ref, wuq_ref, kvg_ref, wukv_ref, bias_ref,
                 cos64_ref, sin64_ref, cosm_ref, sinm_ref,
                 mq_ref, mk_ref, mv_ref, nq_ref, ncmp_ref, nksl_ref, nvsl_ref, nkw_ref, nvw_ref,
                 misc_ref, fq_ref, fk_ref, fv_ref, s5_ref):
    h = h_ref[0]
    tm = h.shape[0]
    u = _rms(h, g_ref[...]).astype(BF16)
    z = _dot(u, win_ref[...])
    lane = lax.broadcasted_iota(jnp.int32, (tm, LANES), 1)

    def rope(x, c, s, half):
        fwd = pltpu.roll(x, LANES - half, 1)
        bwd = pltpu.roll(x, half, 1)
        return x * c + jnp.where(lane % (2 * half) < half, fwd, bwd) * s

    cos64, sin64 = cos64_ref[...], sin64_ref[...]
    cosm, sinm = cosm_ref[...], sinm_ref[...]
    half_m = MLA_ROPE_DIM // 2
    half_n = HEAD_DIM // 2

    cq = _rms(z[:, SEG_CQ:SEG_CKV], qg_ref[...], MLA_Q_LORA).astype(BF16)
    q = _dot(cq, wuq_ref[...])
    ckv = _rms(z[:, SEG_CKV:SEG_KPE], kvg_ref[...]).astype(BF16)
    kv = _dot(ckv, wukv_ref[...])
    kpe = rope(z[:, SEG_KPE:SEG_NQ], cosm, sinm, half_m)
    mla_scale = (MLA_NOPE_DIM + MLA_ROPE_DIM) ** -0.5 * LOG2E
    for hh in range(MLA_HEADS):
        sl = slice(hh * MLA_PAD_DIM, (hh + 1) * MLA_PAD_DIM)
        mq_ref[0, hh] = (rope(q[:, sl], cosm, sinm, half_m) * mla_scale).astype(BF16)
        mk_ref[0, hh] = (kv[:, sl] + kpe).astype(BF16)
        v0 = MLA_HEADS * MLA_PAD_DIM + hh * MLA_V_DIM
        mv_ref[0, hh] = kv[:, v0:v0 + MLA_V_DIM].astype(BF16)

    scale = HEAD_DIM ** -0.5 * LOG2E
    for c in range(NSA_HEADS // 2):
        r = rope(z[:, SEG_NQ + c * LANES:SEG_NQ + (c + 1) * LANES], cos64, sin64, half_n) * scale
        nq_ref[0, 2 * c] = r[:, :HEAD_DIM].astype(BF16)
        nq_ref[0, 2 * c + 1] = r[:, HEAD_DIM:].astype(BF16)
    seg = z[:, SEG_NCMP:SEG_NSLC]
    ncmp_ref[0] = jnp.where(lane < HEAD_DIM, rope(seg, cos64, sin64, half_n), seg)
    seg = z[:, SEG_NSLC:SEG_NSWA]
    nksl_ref[0] = rope(seg, cos64, sin64, half_n)[:, :HEAD_DIM].astype(BF16)
    nvsl_ref[0] = seg[:, HEAD_DIM:].astype(BF16)
    seg = z[:, SEG_NSWA:SEG_MISC]
    nkw_ref[0] = rope(seg, cos64, sin64, half_n)[:, :HEAD_DIM].astype(BF16)
    nvw_ref[0] = seg[:, HEAD_DIM:].astype(BF16)

    x = z[:, SEG_MISC:SEG_FQ] + bias_ref[...]
    log_sig = jnp.minimum(x, 0.0) - jnp.log(1.0 + jnp.exp(-jnp.abs(x)))
    misc_ref[0] = jnp.where(lane < N_GATES, jax.nn.sigmoid(x), log_sig * LOG2E)

    for hh in range(FOX_HEADS):
        sl = slice(hh * HEAD_DIM, (hh + 1) * HEAD_DIM)
        fq_ref[0, hh] = (z[:, SEG_FQ:SEG_FK][:, sl] * scale).astype(BF16)
        fk_ref[0, hh] = z[:, SEG_FK:SEG_FV][:, sl].astype(BF16)
        fv_ref[0, hh] = z[:, SEG_FV:SEG_S5][:, sl].astype(BF16)

    s5_ref[...] = z[:, SEG_S5:IN_PACKED]


def _proj(h, g, win, qg, wuq, kvg, wukv, bias, cos64, sin64, cosm, sinm, layer, tm=512):
    B, S, D = h.shape
    nt = S // tm
    wspec = functools.partial(pl.BlockSpec, pipeline_mode=pl.Buffered(1))

    def w2(shape):
        return wspec((None,) + shape, lambda b, t: (layer, 0, 0))

    def vec(n):
        return pl.BlockSpec((None, 1, n), lambda b, t: (layer, 0, 0))

    def tab():
        return pl.BlockSpec((tm, LANES), lambda b, t: (t, 0))

    def heads(d):
        return pl.BlockSpec((1, 4, tm, d), lambda b, t: (b, 0, t, 0))

    def tok(d):
        return pl.BlockSpec((1, tm, d), lambda b, t: (b, t, 0))

    def hshape(d, dt=BF16):
        return jax.ShapeDtypeStruct((B, 4, S, d), dt)

    def tshape(d, dt):
        return jax.ShapeDtypeStruct((B, S, d), dt)

    return pl.pallas_call(
        _proj_kernel,
        grid=(B, nt),
        in_specs=[
            tok(D), vec(D), w2((D, IN_PACKED)), vec(256), w2((256, 4 * MLA_PAD_DIM)),
            vec(MLA_KV_LORA), w2((MLA_KV_LORA, 4 * MLA_PAD_DIM + 4 * MLA_V_DIM)), vec(LANES),
            tab(), tab(), tab(), tab(),
        ],
        out_specs=[
            heads(MLA_PAD_DIM), heads(MLA_PAD_DIM), heads(MLA_V_DIM),
            heads(HEAD_DIM), tok(LANES), tok(HEAD_DIM), tok(HEAD_DIM), tok(HEAD_DIM), tok(HEAD_DIM),
            tok(LANES), heads(HEAD_DIM), heads(HEAD_DIM), heads(HEAD_DIM),
            pl.BlockSpec((tm, S5_WIDTH), lambda b, t: (t, b)),
        ],
        out_shape=[
            hshape(MLA_PAD_DIM), hshape(MLA_PAD_DIM), hshape(MLA_V_DIM),
            hshape(HEAD_DIM), tshape(LANES, F32),
            tshape(HEAD_DIM, BF16), tshape(HEAD_DIM, BF16), tshape(HEAD_DIM, BF16), tshape(HEAD_DIM, BF16),
            tshape(LANES, F32), hshape(HEAD_DIM), hshape(HEAD_DIM), hshape(HEAD_DIM),
            jax.ShapeDtypeStruct((S, B * S5_WIDTH), F32),
        ],
        compiler_params=_params(("parallel", "parallel")),
        name="mix_proj",
    )(h, g, win, qg, wuq, kvg, wukv, bias, cos64, sin64, cosm, sinm)


def _cumsum_kernel(x_ref, o_ref, *, blk):
    S = x_ref.shape[1]
    r = lax.broadcasted_iota(jnp.int32, (blk, blk), 0)
    c = lax.broadcasted_iota(jnp.int32, (blk, blk), 1)
    tri = jnp.where(r >= c, 1.0, 0.0).astype(BF16)
    carry = jnp.zeros((1, x_ref.shape[2]), F32)
    for i in range(S // blk):
        hi, mid, lo = _split3(x_ref[0, i * blk:(i + 1) * blk, :])
        cs = _dot(tri, hi) + _dot(tri, mid) + _dot(tri, lo) + carry
        o_ref[0, i * blk:(i + 1) * blk, :] = cs
        carry = cs[blk - 1:blk, :]


def _cumsum(x, blk=256):
    B, S, W = x.shape
    return pl.pallas_call(
        functools.partial(_cumsum_kernel, blk=blk),
        grid=(B,),
        in_specs=[pl.BlockSpec((1, S, W), lambda b: (b, 0, 0))],
        out_specs=pl.BlockSpec((1, S, W), lambda b: (b, 0, 0)),
        out_shape=jax.ShapeDtypeStruct((B, S, W), F32),
        compiler_params=_params(("parallel",)),
        name="fox_cumsum",
    )(x)


def _tri_masks(tq):
    r = lax.broadcasted_iota(jnp.int32, (tq, tq), 0)
    c = lax.broadcasted_iota(jnp.int32, (tq, tq), 1)
    keep = jnp.where(r >= c, 1.0, 0.0)
    return keep, (1.0 - keep) * NEG_INF


def _causal_attn_kernel(*refs, tq, has_bias):
    if has_bias:
        q_ref, k_ref, v_ref, cq_ref, ck_ref, o_ref = refs
    else:
        q_ref, k_ref, v_ref, o_ref = refs
    S = q_ref.shape[2]
    keep, bias_tri = _tri_masks(tq)
    for i in range(S // tq):
        lo, hi = i * tq, (i + 1) * tq
        outs = []
        for hh in range(q_ref.shape[1]):
            q = q_ref[0, hh, lo:hi, :]
            s_d = _dot_nt(q, k_ref[0, hh, lo:hi, :]) + bias_tri
            if has_bias:
                cq = cq_ref[0, hh, lo:hi, :]
                s_d = s_d + (cq - ck_ref[0, hh, :, lo:hi])
            m = jnp.max(s_d, axis=-1, keepdims=True)
            if i > 0:
                s_o = _dot_nt(q, k_ref[0, hh, :lo, :])
                if has_bias:
                    s_o = s_o + (cq - ck_ref[0, hh, :, :lo])
                m = jnp.maximum(m, jnp.max(s_o, axis=-1, keepdims=True))
            p_d = jnp.exp2(s_d - m) * keep
            l = jnp.sum(p_d, axis=-1, keepdims=True)
            acc = _dot(p_d.astype(BF16), v_ref[0, hh, lo:hi, :])
            if i > 0:
                p_o = jnp.exp2(s_o - m)
                l = l + jnp.sum(p_o, axis=-1, keepdims=True)
                acc = acc + _dot(p_o.astype(BF16), v_ref[0, hh, :lo, :])
            outs.append(acc / jnp.maximum(l, 1e-30))
        o_ref[0, lo:hi, :] = jnp.concatenate(outs, axis=-1)


def _causal_attn(q, k, v, cum_col=None, cum_row=None, tq=256, hpb=2):
    B, H, S, dk = q.shape
    dv = v.shape[-1]
    has_bias = cum_col is not None

    def heads(w):
        return pl.BlockSpec((1, hpb, S, w), lambda b, h: (b, h, 0, 0))

    in_specs = [heads(dk), heads(dk), heads(dv)]
    args = [q, k, v]
    if has_bias:
        in_specs += [heads(1), pl.BlockSpec((1, hpb, 1, S), lambda b, h: (b, h, 0, 0))]
        args += [cum_col, cum_row]
    return pl.pallas_call(
        functools.partial(_causal_attn_kernel, tq=tq, has_bias=has_bias),
        grid=(B, H // hpb),
        in_specs=in_specs,
        out_specs=pl.BlockSpec((1, S, hpb * dv), lambda b, h: (b, 0, h)),
        out_shape=jax.ShapeDtypeStruct((B, S, H * dv), F32),
        compiler_params=_params(("parallel", "parallel")),
        name="fox_attn" if has_bias else "mla_attn",
    )(*args)


def _nsa_compress_kernel(t_ref, pos_ref, w1_ref, b1_ref, w2_ref, o_ref):
    st = NSA_CMP_STRIDE
    nrow = t_ref.shape[1] // st
    a = [jnp.zeros((nrow, NSA_CMP_HIDDEN), F32)] * 2
    b = [jnp.zeros((nrow, NSA_CMP_HIDDEN), F32)] * 2
    for l in range(st):
        rows = t_ref[0, pl.ds(l, nrow, stride=st), :]
        xa = (rows + pos_ref[l:l + 1, :]).astype(BF16)
        xb = (rows + pos_ref[st + l:st + l + 1, :]).astype(BF16)
        for j in range(2):
            a[j] = a[j] + _dot(xa, w1_ref[j, l * LANES:(l + 1) * LANES, :])
            b[j] = b[j] + _dot(xb, w1_ref[j, (st + l) * LANES:(st + l + 1) * LANES, :])
    for j in range(2):
        hid = jax.nn.gelu(a[j] + pltpu.roll(b[j], nrow - 1, 0) + b1_ref[j])
        o_ref[j, 0] = _dot(hid.astype(BF16), w2_ref[j]).astype(BF16)


def _nsa_compress(t_cmp, pos, w1, b1, w2, layer):
    B, S, _ = t_cmp.shape
    R = S // NSA_CMP_STRIDE
    d = HEAD_DIM

    def par(*shape):
        return pl.BlockSpec((None, 2) + shape, lambda b: (layer, 0, 0, 0))

    return pl.pallas_call(
        _nsa_compress_kernel,
        grid=(B,),
        in_specs=[
            pl.BlockSpec((1, S, LANES), lambda b: (b, 0, 0)),
            pl.BlockSpec((None, NSA_CMP_LEN, LANES), lambda b: (layer, 0, 0)),
            par(NSA_CMP_LEN * LANES, NSA_CMP_HIDDEN), par(1, NSA_CMP_HIDDEN), par(NSA_CMP_HIDDEN, d),
        ],
        out_specs=pl.BlockSpec((2, 1, R, d), lambda b: (0, b, 0, 0)),
        out_shape=jax.ShapeDtypeStruct((2, B, R, d), BF16),
        compiler_params=_params(("parallel",)),
        name="nsa_compress",
    )(t_cmp, pos, w1, b1, w2)


def _nsa_attn_kernel(q_ref, cmp_ref, ksl_ref, vsl_ref, kw_ref, vw_ref, gate_ref, ovl_ref, exp_ref, o_ref, slc_ref, *, tq):
    H = NSA_HEADS
    qi = pl.program_id(1)
    lo = qi * tq
    q4 = q_ref[0].reshape(H * tq, HEAD_DIM)
    pos_q = lo + lax.broadcasted_iota(jnp.int32, (tq, 1), 0)
    lane = lax.broadcasted_iota(jnp.int32, (1, LANES), 1)

    kc, vc = cmp_ref[0, 0], cmp_ref[1, 0]
    nc = kc.shape[0]
    cmp_end = lax.broadcasted_iota(jnp.int32, (1, nc), 1) * NSA_CMP_STRIDE + (NSA_CMP_LEN - 1)
    keep_c = jnp.where(pos_q >= cmp_end, 1.0, 0.0)
    p_c = _masked_softmax2(_dot_nt(q4, kc).reshape(H, tq, nc), keep_c)
    o_cmp = _dot(p_c.reshape(H * tq, nc).astype(BF16), vc).reshape(H, tq, HEAD_DIM)

    nw = NSA_WINDOW + tq
    start = pl.multiple_of(jnp.maximum(lo - NSA_WINDOW, 0), tq)
    kw = kw_ref[0, pl.ds(start, nw), :]
    vw = vw_ref[0, pl.ds(start, nw), :]
    kpos = start + lax.broadcasted_iota(jnp.int32, (1, nw), 1)
    keep_w = jnp.where((kpos <= pos_q) & (kpos > pos_q - NSA_WINDOW), 1.0, 0.0)
    p_w = _masked_softmax2(_dot_nt(q4, kw).reshape(H, tq, nw), keep_w)
    o_swa = _dot(p_w.reshape(H * tq, nw).astype(BF16), vw).reshape(H, tq, HEAD_DIM)

    p_sum = p_c[0] + p_c[1] + p_c[2] + p_c[3]
    hi, mid, lo3 = _split3(p_sum)
    ovl = ovl_ref[...]
    imp = _dot(hi, ovl) + _dot(mid, ovl) + _dot(lo3, ovl)
    cur = pos_q // NSA_SEL_LEN
    forced = (lane == 0) | (lane == cur) | (lane == cur - 1)
    score = jnp.where(forced, NSA_FORCE_SCORE, imp)
    score = jnp.where(lane <= cur, score, NEG_INF)
    n_blk = exp_ref.shape[1] // NSA_SEL_LEN
    score_t = score.T[:n_blk]
    blk = lax.broadcasted_iota(jnp.int32, (n_blk, 1), 0)
    cnt = jnp.zeros(score_t.shape, F32)
    for kk in range(n_blk):
        row = score_t[kk:kk + 1, :]
        beats = (row > score_t) | ((row == score_t) & (blk > kk))
        cnt = cnt + jnp.where(beats, 1.0, 0.0)
    sel_t = jnp.where(cnt < min(NSA_SEL_BLOCKS, n_blk), 1.0, 0.0)
    sel = jnp.concatenate([sel_t, jnp.zeros((LANES - n_blk, tq), F32)], axis=0).T.astype(BF16)

    keep_tri, _ = _tri_masks(tq)

    def slc_variant(i):
        lo_k, hi_k = i * tq, (i + 1) * tq
        keep = _dot(sel, exp_ref[:, :hi_k])
        m = jnp.full((H, tq, 1), NEG_INF, F32)
        l = jnp.zeros((H, tq, 1), F32)
        acc = jnp.zeros((H, tq, HEAD_DIM), F32)
        for c in range(i + 1):
            ks = slice(c * tq, (c + 1) * tq)
            keep_c = keep[:, ks] * keep_tri if c == i else keep[:, ks]
            s = _dot_nt(q4, ksl_ref[0, ks, :]).reshape(H, tq, tq) + ((1.0 - keep_c) * NEG_INF)[None]
            m_new = jnp.maximum(m, jnp.max(s, axis=-1, keepdims=True))
            alpha = jnp.exp2(m - m_new)
            p = jnp.exp2(s - m_new) * keep_c[None]
            l = alpha * l + jnp.sum(p, axis=-1, keepdims=True)
            pv = _dot(p.reshape(H * tq, tq).astype(BF16), vsl_ref[0, ks, :]).reshape(H, tq, HEAD_DIM)
            acc = alpha * acc + pv
            m = m_new
        slc_ref[...] = acc / jnp.maximum(l, 1e-30)

    for i in range(ksl_ref.shape[1] // tq):
        pl.when(qi == i)(functools.partial(slc_variant, i))
    o_slc = slc_ref[...]

    g = gate_ref[0]
    o_ref[0] = jnp.concatenate(
        [g[:, hh:hh + 1] * o_cmp[hh] + g[:, H + hh:H + hh + 1] * o_slc[hh] + g[:, 2 * H + hh:2 * H + hh + 1] * o_swa[hh]
         for hh in range(H)], axis=-1)


def _nsa_attn(q, cmp, ksl, vsl, kw, vw, gates, ovl, expand, tq=256):
    B, H, S, d = q.shape
    R = cmp.shape[2]

    def full(w):
        return pl.BlockSpec((1, S, w), lambda b, i: (b, 0, 0))

    return pl.pallas_call(
        functools.partial(_nsa_attn_kernel, tq=tq),
        grid=(B, S // tq),
        in_specs=[
            pl.BlockSpec((1, H, tq, d), lambda b, i: (b, 0, i, 0)),
            pl.BlockSpec((2, 1, R, d), lambda b, i: (0, b, 0, 0)),
            full(d), full(d), full(d), full(d),
            pl.BlockSpec((1, tq, LANES), lambda b, i: (b, i, 0)),
            pl.BlockSpec((R, LANES), lambda b, i: (0, 0)),
            pl.BlockSpec((LANES, S), lambda b, i: (0, 0)),
        ],
        out_specs=pl.BlockSpec((1, tq, H * d), lambda b, i: (b, i, 0)),
        out_shape=jax.ShapeDtypeStruct((B, S, H * d), F32),
        scratch_shapes=[pltpu.VMEM((H, tq, d), F32)],
        compiler_params=_params(("parallel", "arbitrary")),
        name="nsa_attn",
    )(q, cmp, ksl, vsl, kw, vw, gates, ovl, expand)


def _s5_kernel(u_ref, bre_ref, bim_ref, kr_ref, ki_ref, ar_ref, ai_ref, cre_ref, cim_ref, d_ref, y_ref,
               xr_ref, xi_ref, sr_ref, si_ref, us_ref, *, tc, nb, lane_blk):
    @pl.when(pl.program_id(0) == 0)
    def _():
        sr_ref[...] = jnp.zeros(sr_ref.shape, F32)
        si_ref[...] = jnp.zeros(si_ref.shape, F32)

    W = S5_WIDTH
    ncol = W // LANES
    for b in range(nb):
        for c in range(ncol):
            us_ref[c, pl.ds(b, tc, stride=nb), :] = u_ref[:, b * W + c * LANES:b * W + (c + 1) * LANES]
    u = jnp.concatenate([us_ref[c] for c in range(ncol)], axis=-1)
    ub = u.astype(BF16)
    kr, ki = kr_ref[...], ki_ref[...]
    bre, bim = bre_ref[...], bim_ref[...]
    xr_ref[...] = _dot(ub, (bre * kr - bim * ki).astype(BF16))
    xi_ref[...] = _dot(ub, (bre * ki + bim * kr).astype(BF16))

    for c in range(S5_STATES // lane_blk):
        sl = slice(c * lane_blk, (c + 1) * lane_blk)
        ar = jnp.broadcast_to(ar_ref[:, sl], (nb, lane_blk))
        ai = jnp.broadcast_to(ai_ref[:, sl], (nb, lane_blk))

        def step(t, carry):
            sr, si = carry
            row = pl.ds(pl.multiple_of(t * nb, nb), nb)
            nr = ar * sr - ai * si + xr_ref[row, sl]
            ni = ar * si + ai * sr + xi_ref[row, sl]
            xr_ref[row, sl] = nr
            xi_ref[row, sl] = ni
            return nr, ni

        sr, si = lax.fori_loop(0, tc, step, (sr_ref[:, sl], si_ref[:, sl]), unroll=4)
        sr_ref[:, sl] = sr
        si_ref[:, sl] = si

    y = (_dot(xr_ref[...].astype(BF16), cre_ref[...]) - _dot(xi_ref[...].astype(BF16), cim_ref[...])
         + d_ref[...] * u)
    for c in range(ncol):
        us_ref[c] = y[:, c * LANES:(c + 1) * LANES]
    for b in range(nb):
        for c in range(ncol):
            y_ref[:, b * W + c * LANES:b * W + (c + 1) * LANES] = us_ref[c, pl.ds(b, tc, stride=nb), :]


def _s5_scan(u_t, bre, bim, kr, ki, ar, ai, cre, cim, d, tc=64, lane_blk=256):
    S = u_t.shape[0]
    W = S5_WIDTH
    B = u_t.shape[1] // W
    const = lambda shape: pl.BlockSpec(shape, lambda i: (0, 0))
    return pl.pallas_call(
        functools.partial(_s5_kernel, tc=tc, nb=B, lane_blk=lane_blk),
        grid=(S // tc,),
        in_specs=[
            pl.BlockSpec((tc, B * W), lambda i: (i, 0)),
            const((W, S5_STATES)), const((W, S5_STATES)),
            const((1, S5_STATES)), const((1, S5_STATES)), const((1, S5_STATES)), const((1, S5_STATES)),
            const((S5_STATES, W)), const((S5_STATES, W)), const((1, W)),
        ],
        out_specs=pl.BlockSpec((tc, B * W), lambda i: (i, 0)),
        out_shape=jax.ShapeDtypeStruct((S, B * W), F32),
        scratch_shapes=[
            pltpu.VMEM((tc * B, S5_STATES), F32), pltpu.VMEM((tc * B, S5_STATES), F32),
            pltpu.VMEM((B, S5_STATES), F32), pltpu.VMEM((B, S5_STATES), F32),
            pltpu.VMEM((W // LANES, tc * B, LANES), F32),
        ],
        compiler_params=_params(("arbitrary",)),
        name="s5_scan",
    )(u_t, bre, bim, kr, ki, ar, ai, cre, cim, d)


def _mix_out_kernel(h_ref, om_ref, on_ref, of_ref, ys_ref, gw_ref, gb_ref, gg_ref, wo_ref, pg_ref, o_ref):
    gg = gg_ref[0]
    y = jax.nn.gelu(ys_ref[...])
    y = y * jax.nn.sigmoid(_dot(y.astype(BF16), gw_ref[0]) + gb_ref[...])
    groups = (om_ref[0], on_ref[0], of_ref[0], y)
    cat = jnp.concatenate([_rms(p, gg[i:i + 1, :]).astype(BF16) for i, p in enumerate(groups)], axis=-1)
    o_ref[0] = h_ref[0] + _rms(_dot(cat, wo_ref[0]), pg_ref[...])


def _mix_out(h, o_mla, o_nsa, o_fox, y_s5, glu_w, glu_b, group_g, w_out, post_g, layer, tm=512):
    B, S, D = h.shape

    def group():
        return pl.BlockSpec((1, tm, GROUP_WIDTH), lambda b, t: (b, t, 0))

    return pl.pallas_call(
        _mix_out_kernel,
        grid=(B, S // tm),
        in_specs=[
            pl.BlockSpec((1, tm, D), lambda b, t: (b, t, 0)),
            group(), group(), group(),
            pl.BlockSpec((tm, S5_WIDTH), lambda b, t: (t, b)),
            pl.BlockSpec((1, S5_WIDTH, S5_WIDTH), lambda b, t: (layer, 0, 0)),
            pl.BlockSpec((None, 1, S5_WIDTH), lambda b, t: (layer, 0, 0)),
            pl.BlockSpec((1, 4, GROUP_WIDTH), lambda b, t: (layer, 0, 0)),
            pl.BlockSpec((1, D, D), lambda b, t: (layer, 0, 0)),
            pl.BlockSpec((None, 1, D), lambda b, t: (layer, 0, 0)),
        ],
        out_specs=pl.BlockSpec((1, tm, D), lambda b, t: (b, t, 0)),
        out_shape=jax.ShapeDtypeStruct((B, S, D), F32),
        compiler_params=_params(("parallel", "parallel")),
        name="mix_out",
    )(h, o_mla, o_nsa, o_fox, y_s5, glu_w, glu_b, group_g, w_out, post_g)


def _xa_kv_kernel(mem_ref, g_ref, wkv_ref, kv_ref):
    m = _rms(mem_ref[0], g_ref[...]).astype(BF16)
    kv_ref[0] = _dot(m, wkv_ref[0]).astype(BF16)


def _xa_kv(mem, g, wkv, layer):
    B, M, D = mem.shape
    return pl.pallas_call(
        _xa_kv_kernel,
        grid=(B,),
        in_specs=[
            pl.BlockSpec((1, M, D), lambda b: (b, 0, 0)),
            pl.BlockSpec((None, 1, D), lambda b: (layer, 0, 0)),
            pl.BlockSpec((1, D, 2 * D), lambda b: (layer, 0, 0)),
        ],
        out_specs=pl.BlockSpec((1, M, 2 * D), lambda b: (b, 0, 0)),
        out_shape=jax.ShapeDtypeStruct((B, M, 2 * D), BF16),
        compiler_params=_params(("parallel",)),
        name="xa_kv",
    )(mem, g, wkv)


def _xa_kernel(h_ref, kv_ref, pre_g_ref, wq_ref, wo_ref, post_g_ref, o_ref):
    h = h_ref[0]
    u = _rms(h, pre_g_ref[...]).astype(BF16)
    q = _dot(u, wq_ref[0]) * (XA_HEAD_DIM ** -0.5)
    acc = jnp.zeros(h.shape, F32)
    for hh in range(XA_HEADS):
        sl = slice(hh * XA_HEAD_DIM, (hh + 1) * XA_HEAD_DIM)
        k = kv_ref[0, :, sl]
        v = kv_ref[0, :, D_MODEL + hh * XA_HEAD_DIM:D_MODEL + (hh + 1) * XA_HEAD_DIM]
        s = _dot_nt(q[:, sl].astype(BF16), k)
        m = jnp.max(s, axis=-1, keepdims=True)
        e = jnp.exp(s - m)
        p = e / jnp.sum(e, axis=-1, keepdims=True)
        o = _dot(p.astype(BF16), v)
        acc = acc + _dot(o.astype(BF16), wo_ref[0, sl, :])
    o_ref[0] = h + _rms(acc, post_g_ref[...])


def _xa(h, kv, pre_g, wq, wo, post_g, layer, tm=512):
    B, S, D = h.shape
    M = kv.shape[1]
    return pl.pallas_call(
        _xa_kernel,
        grid=(B, S // tm),
        in_specs=[
            pl.BlockSpec((1, tm, D), lambda b, t: (b, t, 0)),
            pl.BlockSpec((1, M, 2 * D), lambda b, t: (b, 0, 0)),
            pl.BlockSpec((None, 1, D), lambda b, t: (layer, 0, 0)),
            pl.BlockSpec((1, D, D), lambda b, t: (layer, 0, 0)),
            pl.BlockSpec((1, D, D), lambda b, t: (layer, 0, 0)),
            pl.BlockSpec((None, 1, D), lambda b, t: (layer, 0, 0)),
        ],
        out_specs=pl.BlockSpec((1, tm, D), lambda b, t: (b, t, 0)),
        out_shape=jax.ShapeDtypeStruct((B, S, D), F32),
        compiler_params=_params(("parallel", "parallel")),
        name="cross_attn",
    )(h, kv, pre_g, wq, wo, post_g)


def _rope_tables(n, dim):
    inv = 1.0 / (ROPE_THETA ** (jnp.arange(0, dim, 2, dtype=F32) / dim))
    ang = jnp.arange(n, dtype=F32)[:, None] * inv[None, :]
    return jnp.cos(ang), jnp.sin(ang)


def _pack_w_in(w):
    L, D, _ = w.shape
    z = lambda n: jnp.zeros((L, D, n), w.dtype)
    o = 0
    cuts = {}
    for name, n in (("cq", MLA_Q_LORA), ("ckv", MLA_KV_LORA), ("kpe", MLA_ROPE_DIM), ("nq", 256), ("kc", 64),
                    ("vc", 64), ("ksl", 64), ("vsl", 64), ("kw", 64), ("vw", 64), ("gn", N_GATES),
                    ("fq", 256), ("fk", 256), ("fv", 256), ("ff", FOX_HEADS), ("s5", S5_WIDTH)):
        cuts[name] = w[:, :, o:o + n]
        o += n
    c = cuts
    return jnp.concatenate([
        c["cq"], z(64), c["ckv"], z(MLA_NOPE_DIM), c["kpe"], z(32), c["nq"], c["kc"], c["vc"], c["ksl"], c["vsl"],
        c["kw"], c["vw"], c["gn"], c["ff"], z(LANES - N_GATES - FOX_HEADS), c["fq"], c["fk"], c["fv"], c["s5"],
    ], axis=-1)


def kernel(x, mem, ffn1_pre_g, ffn1_w_gate, ffn1_w_up, ffn1_w_down, ffn1_post_g, mix_pre_g, mix_w_in, mla_q_norm_g, mla_w_uq, mla_kv_norm_g, mla_w_ukv, nsa_cmp_pos, nsa_phi_w1, nsa_phi_b1, nsa_phi_w2, nsa_gate_b, fox_f_b, s5_lambda_re, s5_lambda_im, s5_log_dt, s5_b_re, s5_b_im, s5_c_re, s5_c_im, s5_d, s5_glu_w, s5_glu_b, mix_group_g, mix_w_out, mix_post_g, xa_pre_g, xa_mem_g, xa_w_q, xa_w_kv, xa_w_o, xa_post_g, ffn2_pre_g, ffn2_w_gate, ffn2_w_up, ffn2_w_down, ffn2_post_g):
    B, S, D = x.shape
    L = ffn1_pre_g.shape[0]
    T = B * S
    G, P, C = S5_GROUPS, S5_STATE, S5_GROUP
    bf = lambda a: a.astype(BF16)

    w_in = bf(_pack_w_in(mix_w_in))
    qg = jnp.pad(mla_q_norm_g, ((0, 0), (0, 256 - MLA_Q_LORA)))
    wuq = mla_w_uq.reshape(L, MLA_Q_LORA, MLA_HEADS, MLA_NOPE_DIM + MLA_ROPE_DIM)
    wuq = jnp.pad(wuq, ((0, 0), (0, 256 - MLA_Q_LORA), (0, 0), (0, MLA_PAD_DIM - MLA_NOPE_DIM - MLA_ROPE_DIM)))
    wuq = bf(wuq.reshape(L, 256, MLA_HEADS * MLA_PAD_DIM))
    wukv = mla_w_ukv.reshape(L, MLA_KV_LORA, MLA_HEADS, MLA_NOPE_DIM + MLA_V_DIM)
    wk = jnp.pad(wukv[..., :MLA_NOPE_DIM], ((0, 0), (0, 0), (0, 0), (0, MLA_PAD_DIM - MLA_NOPE_DIM)))
    wukv = bf(jnp.concatenate([wk.reshape(L, MLA_KV_LORA, -1), wukv[..., MLA_NOPE_DIM:].reshape(L, MLA_KV_LORA, -1)], axis=-1))
    misc_bias = jnp.concatenate([nsa_gate_b, fox_f_b, jnp.zeros((L, LANES - N_GATES - FOX_HEADS), F32)], axis=-1)

    c32, s32 = _rope_tables(S, HEAD_DIM)
    cos64 = jnp.tile(jnp.concatenate([c32, c32], axis=-1), (1, 2))
    sin64 = jnp.tile(jnp.concatenate([-s32, s32], axis=-1), (1, 2))
    c16, s16 = _rope_tables(S, MLA_ROPE_DIM)
    one, zero = jnp.ones((S, 1), F32), jnp.zeros((S, 1), F32)
    cosm = jnp.concatenate([jnp.tile(one, (1, 64)), c16, c16, jnp.tile(one, (1, 32))], axis=-1)
    sinm = jnp.concatenate([jnp.tile(zero, (1, 64)), -s16, s16, jnp.tile(zero, (1, 32))], axis=-1)

    cmp_pos = jnp.concatenate([nsa_cmp_pos[:, 0], nsa_cmp_pos[:, 1]], axis=-1)
    w1 = nsa_phi_w1.reshape(L, 2, NSA_CMP_LEN, HEAD_DIM, NSA_CMP_HIDDEN)
    w1 = jnp.stack([jnp.pad(w1[:, 0], ((0, 0), (0, 0), (0, HEAD_DIM), (0, 0))),
                    jnp.pad(w1[:, 1], ((0, 0), (0, 0), (HEAD_DIM, 0), (0, 0)))], axis=1)
    phi_w1 = bf(w1.reshape(L, 2, NSA_CMP_LEN * LANES, NSA_CMP_HIDDEN))
    phi_w2 = bf(nsa_phi_w2)
    phi_b1 = nsa_phi_b1.reshape(L, 2, 1, NSA_CMP_HIDDEN)

    n_cmp_pad = S // NSA_CMP_STRIDE
    n_blk = S // NSA_SEL_LEN
    cs = np.arange(n_cmp_pad) * NSA_CMP_STRIDE
    ss = np.arange(LANES) * NSA_SEL_LEN
    ovl_np = ((cs[:, None] < ss[None, :] + NSA_SEL_LEN) & (cs[:, None] + NSA_CMP_LEN > ss[None, :])
              & (np.arange(LANES)[None, :] < n_blk) & (np.arange(n_cmp_pad)[:, None] < n_cmp_pad - 1))
    ovl = jnp.asarray(ovl_np.astype(np.float32), dtype=BF16)
    tq_nsa = 256
    key_blk = np.arange(S) // NSA_SEL_LEN
    expand = jnp.asarray((key_blk[None, :] == np.arange(LANES)[:, None]).astype(np.float32), dtype=BF16)

    dt = jnp.exp(s5_log_dt)[:, :, None]
    lr, li = s5_lambda_re, s5_lambda_im
    mag = jnp.exp(lr * dt)
    a_r, a_i = mag * jnp.cos(li * dt), mag * jnp.sin(li * dt)
    den = lr * lr + li * li
    k_r = ((a_r - 1.0) * lr + a_i * li) / den
    k_i = (a_i * lr - (a_r - 1.0) * li) / den
    flat = lambda a: a.reshape(L, 1, G * P)
    a_r, a_i, k_r, k_i = flat(a_r), flat(a_i), flat(k_r), flat(k_i)
    eye = jnp.eye(G, dtype=F32)
    blk_b = lambda b: jnp.einsum('lgpc,gh->lgchp', b, eye).reshape(L, G * C, G * P)
    blk_c = lambda c: jnp.einsum('lgcp,gh->lgphc', c, eye).reshape(L, G * P, G * C)
    b_re, b_im = blk_b(s5_b_re), blk_b(s5_b_im)
    c_re, c_im = bf(blk_c(s5_c_re)), bf(blk_c(s5_c_im))
    d_skip = s5_d.reshape(L, 1, S5_WIDTH)

    ffn_w = [(bf(ffn1_w_gate), bf(ffn1_w_up), bf(ffn1_w_down)), (bf(ffn2_w_gate), bf(ffn2_w_up), bf(ffn2_w_down))]
    glu_w, w_out = bf(s5_glu_w), bf(mix_w_out)
    wq_x, wkv_x, wo_x = bf(xa_w_q), bf(xa_w_kv), bf(xa_w_o)

    v3 = lambda a: a.reshape(L, 1, a.shape[-1])
    ffn1_pre_g, ffn1_post_g, ffn2_pre_g, ffn2_post_g = v3(ffn1_pre_g), v3(ffn1_post_g), v3(ffn2_pre_g), v3(ffn2_post_g)
    mix_pre_g, mix_post_g, qg, kvg, misc_bias = v3(mix_pre_g), v3(mix_post_g), v3(qg), v3(mla_kv_norm_g), v3(misc_bias)
    s5_glu_b, xa_mem_g, xa_pre_g, xa_post_g = v3(s5_glu_b), v3(xa_mem_g), v3(xa_pre_g), v3(xa_post_g)

    h = x
    for l in range(L):
        h = _ffn(h.reshape(T, D), ffn1_pre_g, *ffn_w[0], ffn1_post_g, l).reshape(B, S, D)

        (mq, mk, mv, nq, ncmp, nksl, nvsl, nkw, nvw, misc, fq, fk, fv, s5u) = _proj(
            h, mix_pre_g, w_in, qg, wuq, kvg, wukv, misc_bias, cos64, sin64, cosm, sinm, l)

        o_mla = _causal_attn(mq, mk, mv)

        cum = _cumsum(misc)[:, :, N_GATES:N_GATES + FOX_HEADS]
        cum_t = jnp.transpose(cum, (0, 2, 1))
        o_fox = _causal_attn(fq, fk, fv, cum_t[..., None], cum_t[:, :, None, :])

        cmp = _nsa_compress(ncmp, cmp_pos, phi_w1, phi_b1, phi_w2, l)
        o_nsa = _nsa_attn(nq, cmp, nksl, nvsl, nkw, nvw, misc, ovl, expand, tq=tq_nsa)

        y_s5 = _s5_scan(s5u, b_re[l], b_im[l], k_r[l], k_i[l], a_r[l], a_i[l],
                        c_re[l], c_im[l], d_skip[l])

        h = _mix_out(h, o_mla, o_nsa, o_fox, y_s5, glu_w, s5_glu_b, mix_group_g, w_out, mix_post_g, l)

        kv = _xa_kv(mem, xa_mem_g, wkv_x, l)
        h = _xa(h, kv, xa_pre_g, wq_x, wo_x, xa_post_g, l)

        h = _ffn(h.reshape(T, D), ffn2_pre_g, *ffn_w[1], ffn2_post_g, l).reshape(B, S, D)
    return h
```

```python
import functools
import math

import numpy as np
import jax
import jax.numpy as jnp
from jax import lax
from jax.experimental import pallas as pl
from jax.experimental.pallas import tpu as pltpu

F32 = jnp.float32
BF16 = jnp.bfloat16

D_MODEL = 1024
HEAD_DIM = 64
ROPE_THETA = 10000.0
NORM_EPS = 1e-6
NEG_INF = -1e30
LOG2E = math.log2(math.e)
MACARON_WEIGHT = 0.5
FFN_DIM = 2816
XA_HEADS = 4
XA_HEAD_DIM = D_MODEL // XA_HEADS

MLA_HEADS = 4
MLA_Q_LORA = 192
MLA_KV_LORA = 128
MLA_NOPE_DIM = 64
MLA_ROPE_DIM = 32
MLA_V_DIM = 64
MLA_PAD_DIM = 128

NSA_HEADS = 4
NSA_CMP_LEN = 32
NSA_CMP_STRIDE = 16
NSA_CMP_HIDDEN = 256
NSA_SEL_LEN = 64
NSA_SEL_BLOCKS = 8
NSA_WINDOW = 256
NSA_FORCE_SCORE = 1e4

FOX_HEADS = 4

S5_WIDTH = 256
S5_GROUP = 16
S5_GROUPS = S5_WIDTH // S5_GROUP
S5_STATE = 64
S5_STATES = S5_GROUPS * S5_STATE

GROUP_WIDTH = 256
LANES = 128
VMEM_LIMIT = 56 * 1024 * 1024

SEG_CQ, SEG_CKV, SEG_KPE, SEG_NQ = 0, 256, 384, 512
SEG_NCMP, SEG_NSLC, SEG_NSWA, SEG_MISC = 768, 896, 1024, 1152
SEG_FQ, SEG_FK, SEG_FV, SEG_S5, IN_PACKED = 1280, 1536, 1792, 2048, 2304
N_GATES = 3 * NSA_HEADS


def _params(sem):
    return pltpu.CompilerParams(dimension_semantics=sem, vmem_limit_bytes=VMEM_LIMIT)


def _rms(x, g, n=None):
    n = x.shape[-1] if n is None else n
    ms = jnp.sum(x * x, axis=-1, keepdims=True) * (1.0 / n)
    return x * lax.rsqrt(ms + NORM_EPS) * g


def _dot(a, b):
    return jnp.dot(a, b, preferred_element_type=F32)


def _dot_nt(a, b):
    return lax.dot_general(a, b, (((1,), (1,)), ((), ())), preferred_element_type=F32)


def _split3(x):
    hi = x.astype(BF16)
    r = x - hi.astype(F32)
    mid = r.astype(BF16)
    lo = (r - mid.astype(F32)).astype(BF16)
    return hi, mid, lo


def _masked_softmax2(s, keep):
    s = s + ((1.0 - keep) * NEG_INF)[None]
    m = jnp.max(s, axis=-1, keepdims=True)
    e = jnp.exp2(s - m) * keep[None]
    return e / jnp.maximum(jnp.sum(e, axis=-1, keepdims=True), 1e-30)


def _ffn_kernel(h_ref, pre_g_ref, wg_ref, wu_ref, wd_ref, post_g_ref, o_ref, *, chunk):
    h = h_ref[...]
    u = _rms(h, pre_g_ref[...]).astype(BF16)
    acc = jnp.zeros(h.shape, F32)
    for c in range(wg_ref.shape[1] // chunk):
        sl = slice(c * chunk, (c + 1) * chunk)
        g = _dot(u, wg_ref[:, sl])
        p = _dot(u, wu_ref[:, sl])
        a = (g * jax.nn.sigmoid(g)) * p
        acc = acc + _dot(a.astype(BF16), wd_ref[sl, :])
    o_ref[...] = h + MACARON_WEIGHT * _rms(acc, post_g_ref[...])


def _ffn(h, pre_g, wg, wu, wd, post_g, layer, tm=512, chunk=256):
    T, D = h.shape
    F = wg.shape[-1]
    wspec = functools.partial(pl.BlockSpec, pipeline_mode=pl.Buffered(1))
    return pl.pallas_call(
        functools.partial(_ffn_kernel, chunk=chunk),
        grid=(T // tm,),
        in_specs=[
            pl.BlockSpec((tm, D), lambda i: (i, 0)),
            pl.BlockSpec((None, 1, D), lambda i: (layer, 0, 0)),
            wspec((None, D, F), lambda i: (layer, 0, 0)),
            wspec((None, D, F), lambda i: (layer, 0, 0)),
            wspec((None, F, D), lambda i: (layer, 0, 0)),
            pl.BlockSpec((None, 1, D), lambda i: (layer, 0, 0)),
        ],
        out_specs=pl.BlockSpec((tm, D), lambda i: (i, 0)),
        out_shape=jax.ShapeDtypeStruct((T, D), F32),
        compiler_params=_params(("parallel",)),
        name="ffn",
    )(h, pre_g, wg, wu, wd, post_g)


def _proj_kernel(h_ref, g_ref, win_ref, qg_ref, wuq_ref, kvg_ref, wukv_ref, bias_ref,
                 cos64_ref, sin64_ref, cosm_ref, sinm_ref,
                 mq_ref, mk_ref, mv_ref, nq_ref, ncmp_ref, nksl_ref, nvsl_ref, nkw_ref, nvw_ref,
                 misc_ref, fq_ref, fk_ref, fv_ref, s5_ref):
    h = h_ref[0]
    tm = h.shape[0]
    u = _rms(h, g_ref[...]).astype(BF16)
    z = _dot(u, win_ref[...])
    lane = lax.broadcasted_iota(jnp.int32, (tm, LANES), 1)

    def rope(x, c, s, half):
        fwd = pltpu.roll(x, LANES - half, 1)
        bwd = pltpu.roll(x, half, 1)
        return x * c + jnp.where(lane % (2 * half) < half, fwd, bwd) * s

    cos64, sin64 = cos64_ref[...], sin64_ref[...]
    cosm, sinm = cosm_ref[...], sinm_ref[...]
    half_m = MLA_ROPE_DIM // 2
    half_n = HEAD_DIM // 2

    cq = _rms(z[:, SEG_CQ:SEG_CKV], qg_ref[...], MLA_Q_LORA).astype(BF16)
    q = _dot(cq, wuq_ref[...])
    ckv = _rms(z[:, SEG_CKV:SEG_KPE], kvg_ref[...]).astype(BF16)
    kv = _dot(ckv, wukv_ref[...])
    kpe = rope(z[:, SEG_KPE:SEG_NQ], cosm, sinm, half_m)
    mla_scale = (MLA_NOPE_DIM + MLA_ROPE_DIM) ** -0.5 * LOG2E
    for hh in range(MLA_HEADS):
        sl = slice(hh * MLA_PAD_DIM, (hh + 1) * MLA_PAD_DIM)
        mq_ref[0, hh] = (rope(q[:, sl], cosm, sinm, half_m) * mla_scale).astype(BF16)
        mk_ref[0, hh] = (kv[:, sl] + kpe).astype(BF16)
        v0 = MLA_HEADS * MLA_PAD_DIM + hh * MLA_V_DIM
        mv_ref[0, hh] = kv[:, v0:v0 + MLA_V_DIM].astype(BF16)

    scale = HEAD_DIM ** -0.5 * LOG2E
    for c in range(NSA_HEADS // 2):
        r = rope(z[:, SEG_NQ + c * LANES:SEG_NQ + (c + 1) * LANES], cos64, sin64, half_n) * scale
        nq_ref[0, 2 * c] = r[:, :HEAD_DIM].astype(BF16)
        nq_ref[0, 2 * c + 1] = r[:, HEAD_DIM:].astype(BF16)
    seg = z[:, SEG_NCMP:SEG_NSLC]
    ncmp_ref[0] = jnp.where(lane < HEAD_DIM, rope(seg, cos64, sin64, half_n), seg)
    seg = z[:, SEG_NSLC:SEG_NSWA]
    nksl_ref[0] = rope(seg, cos64, sin64, half_n)[:, :HEAD_DIM].astype(BF16)
    nvsl_ref[0] = seg[:, HEAD_DIM:].astype(BF16)
    seg = z[:, SEG_NSWA:SEG_MISC]
    nkw_ref[0] = rope(seg, cos64, sin64, half_n)[:, :HEAD_DIM].astype(BF16)
    nvw_ref[0] = seg[:, HEAD_DIM:].astype(BF16)

    x = z[:, SEG_MISC:SEG_FQ] + bias_ref[...]
    log_sig = jnp.minimum(x, 0.0) - jnp.log(1.0 + jnp.exp(-jnp.abs(x)))
    misc_ref[0] = jnp.where(lane < N_GATES, jax.nn.sigmoid(x), log_sig * LOG2E)

    for hh in range(FOX_HEADS):
        sl = slice(hh * HEAD_DIM, (hh + 1) * HEAD_DIM)
        fq_ref[0, hh] = (z[:, SEG_FQ:SEG_FK][:, sl] * scale).astype(BF16)
        fk_ref[0, hh] = z[:, SEG_FK:SEG_FV][:, sl].astype(BF16)
        fv_ref[0, hh] = z[:, SEG_FV:SEG_S5][:, sl].astype(BF16)

    s5_ref[...] = z[:, SEG_S5:IN_PACKED]


def _proj(h, g, win, qg, wuq, kvg, wukv, bias, cos64, sin64, cosm, sinm, layer, tm=1024):
    B, S, D = h.shape
    nt = S // tm
    wspec = functools.partial(pl.BlockSpec, pipeline_mode=pl.Buffered(1))

    def w2(shape):
        return wspec((None,) + shape, lambda b, t: (layer, 0, 0))

    def vec(n):
        return pl.BlockSpec((None, 1, n), lambda b, t: (layer, 0, 0))

    def tab():
        return pl.BlockSpec((tm, LANES), lambda b, t: (t, 0))

    def heads(d):
        return pl.BlockSpec((1, 4, tm, d), lambda b, t: (b, 0, t, 0))

    def tok(d):
        return pl.BlockSpec((1, tm, d), lambda b, t: (b, t, 0))

    def hshape(d, dt=BF16):
        return jax.ShapeDtypeStruct((B, 4, S, d), dt)

    def tshape(d, dt):
        return jax.ShapeDtypeStruct((B, S, d), dt)

    return pl.pallas_call(
        _proj_kernel,
        grid=(B, nt),
        in_specs=[
            tok(D), vec(D), w2((D, IN_PACKED)), vec(256), w2((256, 4 * MLA_PAD_DIM)),
            vec(MLA_KV_LORA), w2((MLA_KV_LORA, 4 * MLA_PAD_DIM + 4 * MLA_V_DIM)), vec(LANES),
            tab(), tab(), tab(), tab(),
        ],
        out_specs=[
            heads(MLA_PAD_DIM), heads(MLA_PAD_DIM), heads(MLA_V_DIM),
            heads(HEAD_DIM), tok(LANES), tok(HEAD_DIM), tok(HEAD_DIM), tok(HEAD_DIM), tok(HEAD_DIM),
            tok(LANES), heads(HEAD_DIM), heads(HEAD_DIM), heads(HEAD_DIM),
            pl.BlockSpec((tm, S5_WIDTH), lambda b, t: (t, b)),
        ],
        out_shape=[
            hshape(MLA_PAD_DIM), hshape(MLA_PAD_DIM), hshape(MLA_V_DIM),
            hshape(HEAD_DIM), tshape(LANES, F32),
            tshape(HEAD_DIM, BF16), tshape(HEAD_DIM, BF16), tshape(HEAD_DIM, BF16), tshape(HEAD_DIM, BF16),
            tshape(LANES, F32), hshape(HEAD_DIM), hshape(HEAD_DIM), hshape(HEAD_DIM),
            jax.ShapeDtypeStruct((S, B * S5_WIDTH), F32),
        ],
        compiler_params=_params(("parallel", "parallel")),
        name="mix_proj",
    )(h, g, win, qg, wuq, kvg, wukv, bias, cos64, sin64, cosm, sinm)


def _cumsum_kernel(x_ref, col_ref, row_ref, *, blk, hpb):
    S = x_ref.shape[1]
    r = lax.broadcasted_iota(jnp.int32, (blk, blk), 0)
    c = lax.broadcasted_iota(jnp.int32, (blk, blk), 1)
    tri = jnp.where(r >= c, 1.0, 0.0).astype(BF16)
    carry = jnp.zeros((1, x_ref.shape[2]), F32)
    for i in range(S // blk):
        rows = slice(i * blk, (i + 1) * blk)
        hi, mid, lo = _split3(x_ref[0, rows, :])
        cs = _dot(tri, hi) + _dot(tri, mid) + _dot(tri, lo) + carry
        carry = cs[blk - 1:blk, :]
        for p in range(FOX_HEADS // hpb):
            shifted = pltpu.roll(cs, LANES - (N_GATES + p * hpb), 1)
            col_ref[0, p, rows, :] = shifted
            row_ref[0, p, :, rows] = shifted.T[:8]


def _cumsum(x, blk=256, hpb=2):
    B, S, W = x.shape
    G = FOX_HEADS // hpb
    return pl.pallas_call(
        functools.partial(_cumsum_kernel, blk=blk, hpb=hpb),
        grid=(B,),
        in_specs=[pl.BlockSpec((1, S, W), lambda b: (b, 0, 0))],
        out_specs=[pl.BlockSpec((1, G, S, W), lambda b: (b, 0, 0, 0)),
                   pl.BlockSpec((1, G, 8, S), lambda b: (b, 0, 0, 0))],
        out_shape=[jax.ShapeDtypeStruct((B, G, S, W), F32), jax.ShapeDtypeStruct((B, G, 8, S), F32)],
        compiler_params=_params(("parallel",)),
        name="fox_cumsum",
    )(x)


def _tri_masks(tq):
    r = lax.broadcasted_iota(jnp.int32, (tq, tq), 0)
    c = lax.broadcasted_iota(jnp.int32, (tq, tq), 1)
    keep = jnp.where(r >= c, 1.0, 0.0)
    return keep, (1.0 - keep) * NEG_INF


def _causal_attn_kernel(*refs, tq, has_bias):
    if has_bias:
        q_ref, k_ref, v_ref, cq_ref, ck_ref, o_ref, s_ref, p_ref, acc_ref = refs
    else:
        q_ref, k_ref, v_ref, o_ref, s_ref, p_ref, acc_ref = refs
    S = q_ref.shape[2]
    n = S // tq
    keep, bias_tri = _tri_masks(tq)
    outs = []
    for hh in range(q_ref.shape[1]):
        for c in range(n):
            lo, hi = c * tq, (c + 1) * tq
            s = _dot_nt(q_ref[0, hh, lo:, :], k_ref[0, hh, lo:hi, :])
            if has_bias:
                s = s + (cq_ref[0, 0, lo:, hh:hh + 1] - ck_ref[0, 0, hh:hh + 1, lo:hi])
            s_ref[lo:hi, lo:hi] = s[:tq] + bias_tri
            if c < n - 1:
                s_ref[hi:, lo:hi] = s[tq:]
        ls = []
        for i in range(n):
            lo, hi = i * tq, (i + 1) * tq
            s_d = s_ref[lo:hi, lo:hi]
            m = jnp.max(s_d, axis=-1, keepdims=True)
            if i > 0:
                s_o = s_ref[lo:hi, :lo]
                m = jnp.maximum(m, jnp.max(s_o, axis=-1, keepdims=True))
            p_d = jnp.exp2(s_d - m) * keep
            l = jnp.sum(p_d, axis=-1, keepdims=True)
            p_ref[lo:hi, lo:hi] = p_d.astype(BF16)
            if i > 0:
                p_o = jnp.exp2(s_o - m)
                l = l + jnp.sum(p_o, axis=-1, keepdims=True)
                p_ref[lo:hi, :lo] = p_o.astype(BF16)
            ls.append(l)
        for c in range(n):
            lo, hi = c * tq, (c + 1) * tq
            pv = _dot(p_ref[lo:, lo:hi], v_ref[0, hh, lo:hi, :])
            if c == 0:
                acc_ref[...] = pv
            else:
                acc_ref[lo:, :] += pv
        outs.append(acc_ref[...] / jnp.maximum(jnp.concatenate(ls, axis=0), 1e-30))
    o_ref[0] = jnp.concatenate(outs, axis=-1)


def _causal_attn(q, k, v, cum_col=None, cum_row=None, tq=256, hpb=2):
    B, H, S, dk = q.shape
    dv = v.shape[-1]
    has_bias = cum_col is not None

    def heads(w):
        return pl.BlockSpec((1, hpb, S, w), lambda b, h: (b, h, 0, 0))

    in_specs = [heads(dk), heads(dk), heads(dv)]
    args = [q, k, v]
    if has_bias:
        in_specs += [pl.BlockSpec((1, 1, S, LANES), lambda b, h: (b, h, 0, 0)),
                     pl.BlockSpec((1, 1, 8, S), lambda b, h: (b, h, 0, 0))]
        args += [cum_col, cum_row]
    return pl.pallas_call(
        functools.partial(_causal_attn_kernel, tq=tq, has_bias=has_bias),
        grid=(B, H // hpb),
        in_specs=in_specs,
        out_specs=pl.BlockSpec((1, S, hpb * dv), lambda b, h: (b, 0, h)),
        out_shape=jax.ShapeDtypeStruct((B, S, H * dv), F32),
        scratch_shapes=[pltpu.VMEM((S, S), F32), pltpu.VMEM((S, S), BF16), pltpu.VMEM((S, dv), F32)],
        compiler_params=_params(("parallel", "parallel")),
        name="fox_attn" if has_bias else "mla_attn",
    )(*args)


def _nsa_compress_kernel(t_ref, pos_ref, w1_ref, b1_ref, w2_ref, o_ref):
    st = NSA_CMP_STRIDE
    nrow = t_ref.shape[1] // st
    a = [jnp.zeros((nrow, NSA_CMP_HIDDEN), F32)] * 2
    b = [jnp.zeros((nrow, NSA_CMP_HIDDEN), F32)] * 2
    for l in range(st):
        rows = t_ref[0, pl.ds(l, nrow, stride=st), :]
        xa = (rows + pos_ref[l:l + 1, :]).astype(BF16)
        xb = (rows + pos_ref[st + l:st + l + 1, :]).astype(BF16)
        for j in range(2):
            a[j] = a[j] + _dot(xa, w1_ref[j, l * LANES:(l + 1) * LANES, :])
            b[j] = b[j] + _dot(xb, w1_ref[j, (st + l) * LANES:(st + l + 1) * LANES, :])
    for j in range(2):
        hid = jax.nn.gelu(a[j] + pltpu.roll(b[j], nrow - 1, 0) + b1_ref[j])
        o_ref[j, 0] = _dot(hid.astype(BF16), w2_ref[j]).astype(BF16)


def _nsa_compress(t_cmp, pos, w1, b1, w2, layer):
    B, S, _ = t_cmp.shape
    R = S // NSA_CMP_STRIDE
    d = HEAD_DIM

    def par(*shape):
        return pl.BlockSpec((None, 2) + shape, lambda b: (layer, 0, 0, 0))

    return pl.pallas_call(
        _nsa_compress_kernel,
        grid=(B,),
        in_specs=[
            pl.BlockSpec((1, S, LANES), lambda b: (b, 0, 0)),
            pl.BlockSpec((None, NSA_CMP_LEN, LANES), lambda b: (layer, 0, 0)),
            par(NSA_CMP_LEN * LANES, NSA_CMP_HIDDEN), par(1, NSA_CMP_HIDDEN), par(NSA_CMP_HIDDEN, d),
        ],
        out_specs=pl.BlockSpec((2, 1, R, d), lambda b: (0, b, 0, 0)),
        out_shape=jax.ShapeDtypeStruct((2, B, R, d), BF16),
        compiler_params=_params(("parallel",)),
        name="nsa_compress",
    )(t_cmp, pos, w1, b1, w2)


def _nsa_attn_kernel(q_ref, cmp_ref, ksl_ref, vsl_ref, kw_ref, vw_ref, gate_ref, ovl_ref, exp_ref, o_ref, slc_ref, *, tq):
    H = NSA_HEADS
    qi = pl.program_id(1)
    lo = qi * tq
    q4 = q_ref[0].reshape(H * tq, HEAD_DIM)
    pos_q = lo + lax.broadcasted_iota(jnp.int32, (tq, 1), 0)
    lane = lax.broadcasted_iota(jnp.int32, (1, LANES), 1)

    kc, vc = cmp_ref[0, 0], cmp_ref[1, 0]
    nc = kc.shape[0]
    cmp_end = lax.broadcasted_iota(jnp.int32, (1, nc), 1) * NSA_CMP_STRIDE + (NSA_CMP_LEN - 1)
    keep_c = jnp.where(pos_q >= cmp_end, 1.0, 0.0)
    p_c = _masked_softmax2(_dot_nt(q4, kc).reshape(H, tq, nc), keep_c)
    o_cmp = _dot(p_c.reshape(H * tq, nc).astype(BF16), vc).reshape(H, tq, HEAD_DIM)

    nw = NSA_WINDOW + tq
    start = pl.multiple_of(jnp.maximum(lo - NSA_WINDOW, 0), tq)
    kw = kw_ref[0, pl.ds(start, nw), :]
    vw = vw_ref[0, pl.ds(start, nw), :]
    kpos = start + lax.broadcasted_iota(jnp.int32, (1, nw), 1)
    keep_w = jnp.where((kpos <= pos_q) & (kpos > pos_q - NSA_WINDOW), 1.0, 0.0)
    p_w = _masked_softmax2(_dot_nt(q4, kw).reshape(H, tq, nw), keep_w)
    o_swa = _dot(p_w.reshape(H * tq, nw).astype(BF16), vw).reshape(H, tq, HEAD_DIM)

    p_sum = p_c[0] + p_c[1] + p_c[2] + p_c[3]
    hi, mid, lo3 = _split3(p_sum)
    ovl = ovl_ref[...]
    imp = _dot(hi, ovl) + _dot(mid, ovl) + _dot(lo3, ovl)
    cur = pos_q // NSA_SEL_LEN
    forced = (lane == 0) | (lane == cur) | (lane == cur - 1)
    score = jnp.where(forced, NSA_FORCE_SCORE, imp)
    score = jnp.where(lane <= cur, score, NEG_INF)
    n_blk = exp_ref.shape[1] // NSA_SEL_LEN
    score_t = score.T[:n_blk]
    blk = lax.broadcasted_iota(jnp.int32, (n_blk, 1), 0)
    cnt = jnp.zeros(score_t.shape, F32)
    for kk in range(n_blk):
        row = score_t[kk:kk + 1, :]
        beats = (row > score_t) | ((row == score_t) & (blk > kk))
        cnt = cnt + jnp.where(beats, 1.0, 0.0)
    sel_t = jnp.where(cnt < min(NSA_SEL_BLOCKS, n_blk), 1.0, 0.0)
    sel = jnp.concatenate([sel_t, jnp.zeros((LANES - n_blk, tq), F32)], axis=0).T.astype(BF16)

    keep_tri, _ = _tri_masks(tq)

    def slc_variant(i):
        lo_k, hi_k = i * tq, (i + 1) * tq
        keep = _dot(sel, exp_ref[:, :hi_k])
        m = jnp.full((H, tq, 1), NEG_INF, F32)
        l = jnp.zeros((H, tq, 1), F32)
        acc = jnp.zeros((H, tq, HEAD_DIM), F32)
        for c in range(i + 1):
            ks = slice(c * tq, (c + 1) * tq)
            keep_c = keep[:, ks] * keep_tri if c == i else keep[:, ks]
            s = _dot_nt(q4, ksl_ref[0, ks, :]).reshape(H, tq, tq) + ((1.0 - keep_c) * NEG_INF)[None]
            m_new = jnp.maximum(m, jnp.max(s, axis=-1, keepdims=True))
            alpha = jnp.exp2(m - m_new)
            p = jnp.exp2(s - m_new) * keep_c[None]
            l = alpha * l + jnp.sum(p, axis=-1, keepdims=True)
            pv = _dot(p.reshape(H * tq, tq).astype(BF16), vsl_ref[0, ks, :]).reshape(H, tq, HEAD_DIM)
            acc = alpha * acc + pv
            m = m_new
        slc_ref[...] = acc / jnp.maximum(l, 1e-30)

    for i in range(ksl_ref.shape[1] // tq):
        pl.when(qi == i)(functools.partial(slc_variant, i))
    o_slc = slc_ref[...]

    g = gate_ref[0]
    o_ref[0] = jnp.concatenate(
        [g[:, hh:hh + 1] * o_cmp[hh] + g[:, H + hh:H + hh + 1] * o_slc[hh] + g[:, 2 * H + hh:2 * H + hh + 1] * o_swa[hh]
         for hh in range(H)], axis=-1)


def _nsa_attn(q, cmp, ksl, vsl, kw, vw, gates, ovl, expand, tq=256):
    B, H, S, d = q.shape
    R = cmp.shape[2]

    def full(w):
        return pl.BlockSpec((1, S, w), lambda b, i: (b, 0, 0))

    return pl.pallas_call(
        functools.partial(_nsa_attn_kernel, tq=tq),
        grid=(B, S // tq),
        in_specs=[
            pl.BlockSpec((1, H, tq, d), lambda b, i: (b, 0, i, 0)),
            pl.BlockSpec((2, 1, R, d), lambda b, i: (0, b, 0, 0)),
            full(d), full(d), full(d), full(d),
            pl.BlockSpec((1, tq, LANES), lambda b, i: (b, i, 0)),
            pl.BlockSpec((R, LANES), lambda b, i: (0, 0)),
            pl.BlockSpec((LANES, S), lambda b, i: (0, 0)),
        ],
        out_specs=pl.BlockSpec((1, tq, H * d), lambda b, i: (b, i, 0)),
        out_shape=jax.ShapeDtypeStruct((B, S, H * d), F32),
        scratch_shapes=[pltpu.VMEM((H, tq, d), F32)],
        compiler_params=_params(("parallel", "arbitrary")),
        name="nsa_attn",
    )(q, cmp, ksl, vsl, kw, vw, gates, ovl, expand)


def _s5_kernel(u_ref, bre_ref, bim_ref, kr_ref, ki_ref, ar_ref, ai_ref, cre_ref, cim_ref, d_ref, y_ref,
               xr_ref, xi_ref, sr_ref, si_ref, us_ref, *, tc, nb, lane_blk):
    @pl.when(pl.program_id(0) == 0)
    def _():
        sr_ref[...] = jnp.zeros(sr_ref.shape, F32)
        si_ref[...] = jnp.zeros(si_ref.shape, F32)

    W = S5_WIDTH
    ncol = W // LANES
    for b in range(nb):
        for c in range(ncol):
            us_ref[c, pl.ds(b, tc, stride=nb), :] = u_ref[:, b * W + c * LANES:b * W + (c + 1) * LANES]
    u = jnp.concatenate([us_ref[c] for c in range(ncol)], axis=-1)
    ub = u.astype(BF16)
    kr, ki = kr_ref[...], ki_ref[...]
    bre, bim = bre_ref[...], bim_ref[...]
    xr_ref[...] = _dot(ub, (bre * kr - bim * ki).astype(BF16))
    xi_ref[...] = _dot(ub, (bre * ki + bim * kr).astype(BF16))

    for c in range(S5_STATES // lane_blk):
        sl = slice(c * lane_blk, (c + 1) * lane_blk)
        ar = jnp.broadcast_to(ar_ref[:, sl], (nb, lane_blk))
        ai = jnp.broadcast_to(ai_ref[:, sl], (nb, lane_blk))

        def step(t, carry):
            sr, si = carry
            row = pl.ds(pl.multiple_of(t * nb, nb), nb)
            nr = ar * sr - ai * si + xr_ref[row, sl]
            ni = ar * si + ai * sr + xi_ref[row, sl]
            xr_ref[row, sl] = nr
            xi_ref[row, sl] = ni
            return nr, ni

        sr, si = lax.fori_loop(0, tc, step, (sr_ref[:, sl], si_ref[:, sl]), unroll=4)
        sr_ref[:, sl] = sr
        si_ref[:, sl] = si

    y = (_dot(xr_ref[...].astype(BF16), cre_ref[...]) - _dot(xi_ref[...].astype(BF16), cim_ref[...])
         + d_ref[...] * u)
    for c in range(ncol):
        us_ref[c] = y[:, c * LANES:(c + 1) * LANES]
    for b in range(nb):
        for c in range(ncol):
            y_ref[:, b * W + c * LANES:b * W + (c + 1) * LANES] = us_ref[c, pl.ds(b, tc, stride=nb), :]


def _s5_scan(u_t, bre, bim, kr, ki, ar, ai, cre, cim, d, tc=64, lane_blk=512):
    S = u_t.shape[0]
    W = S5_WIDTH
    B = u_t.shape[1] // W
    const = lambda shape: pl.BlockSpec(shape, lambda i: (0, 0))
    return pl.pallas_call(
        functools.partial(_s5_kernel, tc=tc, nb=B, lane_blk=lane_blk),
        grid=(S // tc,),
        in_specs=[
            pl.BlockSpec((tc, B * W), lambda i: (i, 0)),
            const((W, S5_STATES)), const((W, S5_STATES)),
            const((1, S5_STATES)), const((1, S5_STATES)), const((1, S5_STATES)), const((1, S5_STATES)),
            const((S5_STATES, W)), const((S5_STATES, W)), const((1, W)),
        ],
        out_specs=pl.BlockSpec((tc, B * W), lambda i: (i, 0)),
        out_shape=jax.ShapeDtypeStruct((S, B * W), F32),
        scratch_shapes=[
            pltpu.VMEM((tc * B, S5_STATES), F32), pltpu.VMEM((tc * B, S5_STATES), F32),
            pltpu.VMEM((B, S5_STATES), F32), pltpu.VMEM((B, S5_STATES), F32),
            pltpu.VMEM((W // LANES, tc * B, LANES), F32),
        ],
        compiler_params=_params(("arbitrary",)),
        name="s5_scan",
    )(u_t, bre, bim, kr, ki, ar, ai, cre, cim, d)


def _mix_out_kernel(h_ref, om_ref, on_ref, of_ref, ys_ref, gw_ref, gb_ref, gg_ref, wo_ref, pg_ref, o_ref):
    gg = gg_ref[0]
    y = jax.nn.gelu(ys_ref[...])
    y = y * jax.nn.sigmoid(_dot(y.astype(BF16), gw_ref[0]) + gb_ref[...])
    groups = (om_ref[0], on_ref[0], of_ref[0], y)
    cat = jnp.concatenate([_rms(p, gg[i:i + 1, :]).astype(BF16) for i, p in enumerate(groups)], axis=-1)
    o_ref[0] = h_ref[0] + _rms(_dot(cat, wo_ref[0]), pg_ref[...])


def _mix_out(h, o_mla, o_nsa, o_fox, y_s5, glu_w, glu_b, group_g, w_out, post_g, layer, tm=1024):
    B, S, D = h.shape

    def group():
        return pl.BlockSpec((1, tm, GROUP_WIDTH), lambda b, t: (b, t, 0))

    return pl.pallas_call(
        _mix_out_kernel,
        grid=(B, S // tm),
        in_specs=[
            pl.BlockSpec((1, tm, D), lambda b, t: (b, t, 0)),
            group(), group(), group(),
            pl.BlockSpec((tm, S5_WIDTH), lambda b, t: (t, b)),
            pl.BlockSpec((1, S5_WIDTH, S5_WIDTH), lambda b, t: (layer, 0, 0)),
            pl.BlockSpec((None, 1, S5_WIDTH), lambda b, t: (layer, 0, 0)),
            pl.BlockSpec((1, 4, GROUP_WIDTH), lambda b, t: (layer, 0, 0)),
            pl.BlockSpec((1, D, D), lambda b, t: (layer, 0, 0)),
            pl.BlockSpec((None, 1, D), lambda b, t: (layer, 0, 0)),
        ],
        out_specs=pl.BlockSpec((1, tm, D), lambda b, t: (b, t, 0)),
        out_shape=jax.ShapeDtypeStruct((B, S, D), F32),
        compiler_params=_params(("parallel", "parallel")),
        name="mix_out",
    )(h, o_mla, o_nsa, o_fox, y_s5, glu_w, glu_b, group_g, w_out, post_g)


def _xa_kv_kernel(mem_ref, g_ref, wkv_ref, kv_ref):
    m = _rms(mem_ref[0], g_ref[...]).astype(BF16)
    kv_ref[0] = _dot(m, wkv_ref[0]).astype(BF16)


def _xa_kv(mem, g, wkv, layer):
    B, M, D = mem.shape
    return pl.pallas_call(
        _xa_kv_kernel,
        grid=(B,),
        in_specs=[
            pl.BlockSpec((1, M, D), lambda b: (b, 0, 0)),
            pl.BlockSpec((None, 1, D), lambda b: (layer, 0, 0)),
            pl.BlockSpec((1, D, 2 * D), lambda b: (layer, 0, 0)),
        ],
        out_specs=pl.BlockSpec((1, M, 2 * D), lambda b: (b, 0, 0)),
        out_shape=jax.ShapeDtypeStruct((B, M, 2 * D), BF16),
        compiler_params=_params(("parallel",)),
        name="xa_kv",
    )(mem, g, wkv)


def _xa_kernel(h_ref, kv_ref, pre_g_ref, wq_ref, wo_ref, post_g_ref, o_ref):
    h = h_ref[0]
    u = _rms(h, pre_g_ref[...]).astype(BF16)
    q = _dot(u, wq_ref[0]) * (XA_HEAD_DIM ** -0.5)
    acc = jnp.zeros(h.shape, F32)
    for hh in range(XA_HEADS):
        sl = slice(hh * XA_HEAD_DIM, (hh + 1) * XA_HEAD_DIM)
        k = kv_ref[0, :, sl]
        v = kv_ref[0, :, D_MODEL + hh * XA_HEAD_DIM:D_MODEL + (hh + 1) * XA_HEAD_DIM]
        s = _dot_nt(q[:, sl].astype(BF16), k)
        m = jnp.max(s, axis=-1, keepdims=True)
        e = jnp.exp(s - m)
        p = e / jnp.sum(e, axis=-1, keepdims=True)
        o = _dot(p.astype(BF16), v)
        acc = acc + _dot(o.astype(BF16), wo_ref[0, sl, :])
    o_ref[0] = h + _rms(acc, post_g_ref[...])


def _xa(h, kv, pre_g, wq, wo, post_g, layer, tm=1024):
    B, S, D = h.shape
    M = kv.shape[1]
    return pl.pallas_call(
        _xa_kernel,
        grid=(B, S // tm),
        in_specs=[
            pl.BlockSpec((1, tm, D), lambda b, t: (b, t, 0)),
            pl.BlockSpec((1, M, 2 * D), lambda b, t: (b, 0, 0)),
            pl.BlockSpec((None, 1, D), lambda b, t: (layer, 0, 0)),
            pl.BlockSpec((1, D, D), lambda b, t: (layer, 0, 0)),
            pl.BlockSpec((1, D, D), lambda b, t: (layer, 0, 0)),
            pl.BlockSpec((None, 1, D), lambda b, t: (layer, 0, 0)),
        ],
        out_specs=pl.BlockSpec((1, tm, D), lambda b, t: (b, t, 0)),
        out_shape=jax.ShapeDtypeStruct((B, S, D), F32),
        compiler_params=_params(("parallel", "parallel")),
        name="cross_attn",
    )(h, kv, pre_g, wq, wo, post_g)


def _rope_tables(n, dim):
    inv = 1.0 / (ROPE_THETA ** (jnp.arange(0, dim, 2, dtype=F32) / dim))
    ang = jnp.arange(n, dtype=F32)[:, None] * inv[None, :]
    return jnp.cos(ang), jnp.sin(ang)


def _pack_w_in(w):
    L, D, _ = w.shape
    z = lambda n: jnp.zeros((L, D, n), w.dtype)
    o = 0
    cuts = {}
    for name, n in (("cq", MLA_Q_LORA), ("ckv", MLA_KV_LORA), ("kpe", MLA_ROPE_DIM), ("nq", 256), ("kc", 64),
                    ("vc", 64), ("ksl", 64), ("vsl", 64), ("kw", 64), ("vw", 64), ("gn", N_GATES),
                    ("fq", 256), ("fk", 256), ("fv", 256), ("ff", FOX_HEADS), ("s5", S5_WIDTH)):
        cuts[name] = w[:, :, o:o + n]
        o += n
    c = cuts
    return jnp.concatenate([
        c["cq"], z(64), c["ckv"], z(MLA_NOPE_DIM), c["kpe"], z(32), c["nq"], c["kc"], c["vc"], c["ksl"], c["vsl"],
        c["kw"], c["vw"], c["gn"], c["ff"], z(LANES - N_GATES - FOX_HEADS), c["fq"], c["fk"], c["fv"], c["s5"],
    ], axis=-1)


def kernel(x, mem, ffn1_pre_g, ffn1_w_gate, ffn1_w_up, ffn1_w_down, ffn1_post_g, mix_pre_g, mix_w_in, mla_q_norm_g, mla_w_uq, mla_kv_norm_g, mla_w_ukv, nsa_cmp_pos, nsa_phi_w1, nsa_phi_b1, nsa_phi_w2, nsa_gate_b, fox_f_b, s5_lambda_re, s5_lambda_im, s5_log_dt, s5_b_re, s5_b_im, s5_c_re, s5_c_im, s5_d, s5_glu_w, s5_glu_b, mix_group_g, mix_w_out, mix_post_g, xa_pre_g, xa_mem_g, xa_w_q, xa_w_kv, xa_w_o, xa_post_g, ffn2_pre_g, ffn2_w_gate, ffn2_w_up, ffn2_w_down, ffn2_post_g):
    B, S, D = x.shape
    L = ffn1_pre_g.shape[0]
    T = B * S
    G, P, C = S5_GROUPS, S5_STATE, S5_GROUP
    bf = lambda a: a.astype(BF16)

    w_in = bf(_pack_w_in(mix_w_in))
    qg = jnp.pad(mla_q_norm_g, ((0, 0), (0, 256 - MLA_Q_LORA)))
    wuq = mla_w_uq.reshape(L, MLA_Q_LORA, MLA_HEADS, MLA_NOPE_DIM + MLA_ROPE_DIM)
    wuq = jnp.pad(wuq, ((0, 0), (0, 256 - MLA_Q_LORA), (0, 0), (0, MLA_PAD_DIM - MLA_NOPE_DIM - MLA_ROPE_DIM)))
    wuq = bf(wuq.reshape(L, 256, MLA_HEADS * MLA_PAD_DIM))
    wukv = mla_w_ukv.reshape(L, MLA_KV_LORA, MLA_HEADS, MLA_NOPE_DIM + MLA_V_DIM)
    wk = jnp.pad(wukv[..., :MLA_NOPE_DIM], ((0, 0), (0, 0), (0, 0), (0, MLA_PAD_DIM - MLA_NOPE_DIM)))
    wukv = bf(jnp.concatenate([wk.reshape(L, MLA_KV_LORA, -1), wukv[..., MLA_NOPE_DIM:].reshape(L, MLA_KV_LORA, -1)], axis=-1))
    misc_bias = jnp.concatenate([nsa_gate_b, fox_f_b, jnp.zeros((L, LANES - N_GATES - FOX_HEADS), F32)], axis=-1)

    c32, s32 = _rope_tables(S, HEAD_DIM)
    cos64 = jnp.tile(jnp.concatenate([c32, c32], axis=-1), (1, 2))
    sin64 = jnp.tile(jnp.concatenate([-s32, s32], axis=-1), (1, 2))
    c16, s16 = _rope_tables(S, MLA_ROPE_DIM)
    one, zero = jnp.ones((S, 1), F32), jnp.zeros((S, 1), F32)
    cosm = jnp.concatenate([jnp.tile(one, (1, 64)), c16, c16, jnp.tile(one, (1, 32))], axis=-1)
    sinm = jnp.concatenate([jnp.tile(zero, (1, 64)), -s16, s16, jnp.tile(zero, (1, 32))], axis=-1)

    cmp_pos = jnp.concatenate([nsa_cmp_pos[:, 0], nsa_cmp_pos[:, 1]], axis=-1)
    w1 = nsa_phi_w1.reshape(L, 2, NSA_CMP_LEN, HEAD_DIM, NSA_CMP_HIDDEN)
    w1 = jnp.stack([jnp.pad(w1[:, 0], ((0, 0), (0, 0), (0, HEAD_DIM), (0, 0))),
                    jnp.pad(w1[:, 1], ((0, 0), (0, 0), (HEAD_DIM, 0), (0, 0)))], axis=1)
    phi_w1 = bf(w1.reshape(L, 2, NSA_CMP_LEN * LANES, NSA_CMP_HIDDEN))
    phi_w2 = bf(nsa_phi_w2)
    phi_b1 = nsa_phi_b1.reshape(L, 2, 1, NSA_CMP_HIDDEN)

    n_cmp_pad = S // NSA_CMP_STRIDE
    n_blk = S // NSA_SEL_LEN
    cs = np.arange(n_cmp_pad) * NSA_CMP_STRIDE
    ss = np.arange(LANES) * NSA_SEL_LEN
    ovl_np = ((cs[:, None] < ss[None, :] + NSA_SEL_LEN) & (cs[:, None] + NSA_CMP_LEN > ss[None, :])
              & (np.arange(LANES)[None, :] < n_blk) & (np.arange(n_cmp_pad)[:, None] < n_cmp_pad - 1))
    ovl = jnp.asarray(ovl_np.astype(np.float32), dtype=BF16)
    tq_nsa = 256
    key_blk = np.arange(S) // NSA_SEL_LEN
    expand = jnp.asarray((key_blk[None, :] == np.arange(LANES)[:, None]).astype(np.float32), dtype=BF16)

    dt = jnp.exp(s5_log_dt)[:, :, None]
    lr, li = s5_lambda_re, s5_lambda_im
    mag = jnp.exp(lr * dt)
    a_r, a_i = mag * jnp.cos(li * dt), mag * jnp.sin(li * dt)
    den = lr * lr + li * li
    k_r = ((a_r - 1.0) * lr + a_i * li) / den
    k_i = (a_i * lr - (a_r - 1.0) * li) / den
    flat = lambda a: a.reshape(L, 1, G * P)
    a_r, a_i, k_r, k_i = flat(a_r), flat(a_i), flat(k_r), flat(k_i)
    eye = jnp.eye(G, dtype=F32)
    blk_b = lambda b: jnp.einsum('lgpc,gh->lgchp', b, eye).reshape(L, G * C, G * P)
    blk_c = lambda c: jnp.einsum('lgcp,gh->lgphc', c, eye).reshape(L, G * P, G * C)
    b_re, b_im = blk_b(s5_b_re), blk_b(s5_b_im)
    c_re, c_im = bf(blk_c(s5_c_re)), bf(blk_c(s5_c_im))
    d_skip = s5_d.reshape(L, 1, S5_WIDTH)

    ffn_w = [(bf(ffn1_w_gate), bf(ffn1_w_up), bf(ffn1_w_down)), (bf(ffn2_w_gate), bf(ffn2_w_up), bf(ffn2_w_down))]
    glu_w, w_out = bf(s5_glu_w), bf(mix_w_out)
    wq_x, wkv_x, wo_x = bf(xa_w_q), bf(xa_w_kv), bf(xa_w_o)

    v3 = lambda a: a.reshape(L, 1, a.shape[-1])
    ffn1_pre_g, ffn1_post_g, ffn2_pre_g, ffn2_post_g = v3(ffn1_pre_g), v3(ffn1_post_g), v3(ffn2_pre_g), v3(ffn2_post_g)
    mix_pre_g, mix_post_g, qg, kvg, misc_bias = v3(mix_pre_g), v3(mix_post_g), v3(qg), v3(mla_kv_norm_g), v3(misc_bias)
    s5_glu_b, xa_mem_g, xa_pre_g, xa_post_g = v3(s5_glu_b), v3(xa_mem_g), v3(xa_pre_g), v3(xa_post_g)

    h = x
    for l in range(L):
        h = _ffn(h.reshape(T, D), ffn1_pre_g, *ffn_w[0], ffn1_post_g, l).reshape(B, S, D)

        (mq, mk, mv, nq, ncmp, nksl, nvsl, nkw, nvw, misc, fq, fk, fv, s5u) = _proj(
            h, mix_pre_g, w_in, qg, wuq, kvg, wukv, misc_bias, cos64, sin64, cosm, sinm, l)

        o_mla = _causal_attn(mq, mk, mv)

        cum_col, cum_row = _cumsum(misc)
        o_fox = _causal_attn(fq, fk, fv, cum_col, cum_row)

        cmp = _nsa_compress(ncmp, cmp_pos, phi_w1, phi_b1, phi_w2, l)
        o_nsa = _nsa_attn(nq, cmp, nksl, nvsl, nkw, nvw, misc, ovl, expand, tq=tq_nsa)

        y_s5 = _s5_scan(s5u, b_re[l], b_im[l], k_r[l], k_i[l], a_r[l], a_i[l],
                        c_re[l], c_im[l], d_skip[l])

        h = _mix_out(h, o_mla, o_nsa, o_fox, y_s5, glu_w, s5_glu_b, mix_group_g, w_out, mix_post_g, l)

        kv = _xa_kv(mem, xa_mem_g, wkv_x, l)
        h = _xa(h, kv, xa_pre_g, wq_x, wo_x, xa_post_g, l)

        h = _ffn(h.reshape(T, D), ffn2_pre_g, *ffn_w[1], ffn2_post_g, l).reshape(B, S, D)
    return h
```

```python
import functools
import math

import numpy as np
import jax
import jax.numpy as jnp
from jax import lax
from jax.experimental import pallas as pl
from jax.experimental.pallas import tpu as pltpu

F32 = jnp.float32
BF16 = jnp.bfloat16

D_MODEL = 1024
HEAD_DIM = 64
ROPE_THETA = 10000.0
NORM_EPS = 1e-6
NEG_INF = -1e30
LOG2E = math.log2(math.e)
MACARON_WEIGHT = 0.5
FFN_DIM = 2816
XA_HEADS = 4
XA_HEAD_DIM = D_MODEL // XA_HEADS

MLA_HEADS = 4
MLA_Q_LORA = 192
MLA_KV_LORA = 128
MLA_NOPE_DIM = 64
MLA_ROPE_DIM = 32
MLA_V_DIM = 64
MLA_PAD_DIM = 128

NSA_HEADS = 4
NSA_CMP_LEN = 32
NSA_CMP_STRIDE = 16
NSA_CMP_HIDDEN = 256
NSA_SEL_LEN = 64
NSA_SEL_BLOCKS = 8
NSA_WINDOW = 256
NSA_FORCE_SCORE = 1e4

FOX_HEADS = 4

S5_WIDTH = 256
S5_GROUP = 16
S5_GROUPS = S5_WIDTH // S5_GROUP
S5_STATE = 64
S5_STATES = S5_GROUPS * S5_STATE

GROUP_WIDTH = 256
LANES = 128
VMEM_LIMIT = 56 * 1024 * 1024

SEG_CQ, SEG_CKV, SEG_KPE, SEG_NQ = 0, 256, 384, 512
SEG_NCMP, SEG_NSLC, SEG_NSWA, SEG_MISC = 768, 896, 1024, 1152
SEG_FQ, SEG_FK, SEG_FV, SEG_S5, IN_PACKED = 1280, 1536, 1792, 2048, 2304
N_GATES = 3 * NSA_HEADS


def _params(sem):
    return pltpu.CompilerParams(dimension_semantics=sem, vmem_limit_bytes=VMEM_LIMIT)


def _rms(x, g, n=None):
    n = x.shape[-1] if n is None else n
    ms = jnp.sum(x * x, axis=-1, keepdims=True) * (1.0 / n)
    return x * lax.rsqrt(ms + NORM_EPS) * g


def _dot(a, b):
    return jnp.dot(a, b, preferred_element_type=F32)


def _dot_nt(a, b):
    return lax.dot_general(a, b, (((1,), (1,)), ((), ())), preferred_element_type=F32)


def _split3(x):
    hi = x.astype(BF16)
    r = x - hi.astype(F32)
    mid = r.astype(BF16)
    lo = (r - mid.astype(F32)).astype(BF16)
    return hi, mid, lo


def _masked_softmax2(s, keep):
    s = s + ((1.0 - keep) * NEG_INF)[None]
    m = jnp.max(s, axis=-1, keepdims=True)
    e = jnp.exp2(s - m) * keep[None]
    return e / jnp.maximum(jnp.sum(e, axis=-1, keepdims=True), 1e-30)


def _ffn_kernel(h_ref, pre_g_ref, wg_ref, wu_ref, wd_ref, post_g_ref, o_ref, *, chunk):
    h = h_ref[...]
    u = _rms(h, pre_g_ref[...]).astype(BF16)
    acc = jnp.zeros(h.shape, F32)
    for c in range(wg_ref.shape[1] // chunk):
        sl = slice(c * chunk, (c + 1) * chunk)
        g = _dot(u, wg_ref[:, sl])
        p = _dot(u, wu_ref[:, sl])
        a = (g * jax.nn.sigmoid(g)) * p
        acc = acc + _dot(a.astype(BF16), wd_ref[sl, :])
    o_ref[...] = h + MACARON_WEIGHT * _rms(acc, post_g_ref[...])


def _ffn(h, pre_g, wg, wu, wd, post_g, layer, tm=512, chunk=256):
    T, D = h.shape
    F = wg.shape[-1]
    wspec = functools.partial(pl.BlockSpec, pipeline_mode=pl.Buffered(1))
    return pl.pallas_call(
        functools.partial(_ffn_kernel, chunk=chunk),
        grid=(T // tm,),
        in_specs=[
            pl.BlockSpec((tm, D), lambda i: (i, 0)),
            pl.BlockSpec((None, 1, D), lambda i: (layer, 0, 0)),
            wspec((None, D, F), lambda i: (layer, 0, 0)),
            wspec((None, D, F), lambda i: (layer, 0, 0)),
            wspec((None, F, D), lambda i: (layer, 0, 0)),
            pl.BlockSpec((None, 1, D), lambda i: (layer, 0, 0)),
        ],
        out_specs=pl.BlockSpec((tm, D), lambda i: (i, 0)),
        out_shape=jax.ShapeDtypeStruct((T, D), F32),
        compiler_params=_params(("parallel",)),
        name="ffn",
    )(h, pre_g, wg, wu, wd, post_g)


def _proj_kernel(h_ref, g_ref, win_ref, qg_ref, wuq_ref, kvg_ref, wukv_ref, bias_ref,
                 cos64_ref, sin64_ref, cosm_ref, sinm_ref,
                 mq_ref, mk_ref, mv_ref, nq_ref, ncmp_ref, nksl_ref, nvsl_ref, nkw_ref, nvw_ref,
                 misc_ref, fq_ref, fk_ref, fv_ref, s5_ref):
    h = h_ref[0]
    tm = h.shape[0]
    u = _rms(h, g_ref[...]).astype(BF16)
    z = _dot(u, win_ref[...])
    lane = lax.broadcasted_iota(jnp.int32, (tm, LANES), 1)

    def rope(x, c, s, half):
        fwd = pltpu.roll(x, LANES - half, 1)
        bwd = pltpu.roll(x, half, 1)
        return x * c + jnp.where(lane % (2 * half) < half, fwd, bwd) * s

    def padded(x, j, fill):
        blk = x[:, (j // 2) * LANES:(j // 2 + 1) * LANES]
        if j % 2:
            blk = pltpu.roll(blk, HEAD_DIM, 1)
        return jnp.where(lane < HEAD_DIM, blk, fill).astype(BF16)

    def with_ones(x, j):
        return padded(x, j, 1.0)

    cos64, sin64 = cos64_ref[...], sin64_ref[...]
    cosm, sinm = cosm_ref[...], sinm_ref[...]
    half_m = MLA_ROPE_DIM // 2
    half_n = HEAD_DIM // 2

    cq = _rms(z[:, SEG_CQ:SEG_CKV], qg_ref[...], MLA_Q_LORA).astype(BF16)
    q = _dot(cq, wuq_ref[...])
    ckv = _rms(z[:, SEG_CKV:SEG_KPE], kvg_ref[...]).astype(BF16)
    kv = _dot(ckv, wukv_ref[...])
    kpe = rope(z[:, SEG_KPE:SEG_NQ], cosm, sinm, half_m)
    mla_scale = (MLA_NOPE_DIM + MLA_ROPE_DIM) ** -0.5 * LOG2E
    for hh in range(MLA_HEADS):
        sl = slice(hh * MLA_PAD_DIM, (hh + 1) * MLA_PAD_DIM)
        mq_ref[0, hh] = (rope(q[:, sl], cosm, sinm, half_m) * mla_scale).astype(BF16)
        mk_ref[0, hh] = (kv[:, sl] + kpe).astype(BF16)
        mv_ref[0, hh] = with_ones(kv[:, MLA_HEADS * MLA_PAD_DIM:], hh)

    scale = HEAD_DIM ** -0.5 * LOG2E
    for c in range(NSA_HEADS // 2):
        r = rope(z[:, SEG_NQ + c * LANES:SEG_NQ + (c + 1) * LANES], cos64, sin64, half_n) * scale
        nq_ref[0, 2 * c] = r[:, :HEAD_DIM].astype(BF16)
        nq_ref[0, 2 * c + 1] = r[:, HEAD_DIM:].astype(BF16)
    seg = z[:, SEG_NCMP:SEG_NSLC]
    ncmp_ref[0] = jnp.where(lane < HEAD_DIM, rope(seg, cos64, sin64, half_n), seg)
    seg = z[:, SEG_NSLC:SEG_NSWA]
    nksl_ref[0] = rope(seg, cos64, sin64, half_n)[:, :HEAD_DIM].astype(BF16)
    nvsl_ref[0] = with_ones(seg, 1)
    seg = z[:, SEG_NSWA:SEG_MISC]
    nkw_ref[0] = rope(seg, cos64, sin64, half_n)[:, :HEAD_DIM].astype(BF16)
    nvw_ref[0] = with_ones(seg, 1)

    x = z[:, SEG_MISC:SEG_FQ] + bias_ref[...]
    log_sig = jnp.minimum(x, 0.0) - jnp.log(1.0 + jnp.exp(-jnp.abs(x)))
    misc_ref[0] = jnp.where(lane < N_GATES, jax.nn.sigmoid(x), log_sig * LOG2E)

    for hh in range(FOX_HEADS):
        fq_ref[0, hh] = padded(z[:, SEG_FQ:SEG_FK] * scale, hh, 0.0)
        fk_ref[0, hh] = padded(z[:, SEG_FK:SEG_FV], hh, 0.0)
        fv_ref[0, hh] = with_ones(z[:, SEG_FV:SEG_S5], hh)

    s5_ref[...] = z[:, SEG_S5:IN_PACKED]


def _proj(h, g, win, qg, wuq, kvg, wukv, bias, cos64, sin64, cosm, sinm, layer, tm=1024):
    B, S, D = h.shape
    nt = S // tm
    wspec = functools.partial(pl.BlockSpec, pipeline_mode=pl.Buffered(1))

    def w2(shape):
        return wspec((None,) + shape, lambda b, t: (layer, 0, 0))

    def vec(n):
        return pl.BlockSpec((None, 1, n), lambda b, t: (layer, 0, 0))

    def tab():
        return pl.BlockSpec((tm, LANES), lambda b, t: (t, 0))

    def heads(d):
        return pl.BlockSpec((1, 4, tm, d), lambda b, t: (b, 0, t, 0))

    def tok(d):
        return pl.BlockSpec((1, tm, d), lambda b, t: (b, t, 0))

    def hshape(d, dt=BF16):
        return jax.ShapeDtypeStruct((B, 4, S, d), dt)

    def tshape(d, dt):
        return jax.ShapeDtypeStruct((B, S, d), dt)

    return pl.pallas_call(
        _proj_kernel,
        grid=(B, nt),
        in_specs=[
            tok(D), vec(D), w2((D, IN_PACKED)), vec(256), w2((256, 4 * MLA_PAD_DIM)),
            vec(MLA_KV_LORA), w2((MLA_KV_LORA, 4 * MLA_PAD_DIM + 4 * MLA_V_DIM)), vec(LANES),
            tab(), tab(), tab(), tab(),
        ],
        out_specs=[
            heads(MLA_PAD_DIM), heads(MLA_PAD_DIM), heads(LANES),
            heads(HEAD_DIM), tok(LANES), tok(HEAD_DIM), tok(LANES), tok(HEAD_DIM), tok(LANES),
            tok(LANES), heads(LANES), heads(LANES), heads(LANES),
            pl.BlockSpec((tm, S5_WIDTH), lambda b, t: (t, b)),
        ],
        out_shape=[
            hshape(MLA_PAD_DIM), hshape(MLA_PAD_DIM), hshape(LANES),
            hshape(HEAD_DIM), tshape(LANES, F32),
            tshape(HEAD_DIM, BF16), tshape(LANES, BF16), tshape(HEAD_DIM, BF16), tshape(LANES, BF16),
            tshape(LANES, F32), hshape(LANES), hshape(LANES), hshape(LANES),
            jax.ShapeDtypeStruct((S, B * S5_WIDTH), F32),
        ],
        compiler_params=_params(("parallel", "parallel")),
        name="mix_proj",
    )(h, g, win, qg, wuq, kvg, wukv, bias, cos64, sin64, cosm, sinm)


def _fox_bias_kernel(x_ref, q_ref, k_ref, qa_ref, ka_ref, *, blk):
    S = x_ref.shape[1]
    d = HEAD_DIM
    r = lax.broadcasted_iota(jnp.int32, (blk, blk), 0)
    c = lax.broadcasted_iota(jnp.int32, (blk, blk), 1)
    tri = jnp.where(r >= c, 1.0, 0.0).astype(BF16)
    lane = lax.broadcasted_iota(jnp.int32, (blk, LANES), 1)
    first, second = (lane >= d) & (lane < d + 12), (lane >= d + 12) & (lane < d + 24)
    carry = jnp.zeros((1, x_ref.shape[2]), F32)
    for i in range(S // blk):
        rows = slice(i * blk, (i + 1) * blk)
        hi, mid, lo = _split3(x_ref[0, rows, :])
        cs = _dot(tri, hi) + _dot(tri, mid) + _dot(tri, lo) + carry
        carry = cs[blk - 1:blk, :]
        val = jnp.zeros((blk, LANES), F32)
        for j, p in enumerate(_split3(cs)):
            grp = (lane >= d + 4 * j) & (lane < d + 4 * j + 4)
            val = jnp.where(grp, pltpu.roll(p.astype(F32), d + 4 * j - N_GATES, 1), val)
        qa = jnp.where(second, 1.0, val).astype(BF16)
        ka = jnp.where(first, 1.0, jnp.where(second, -pltpu.roll(val, 12, 1), 0.0))
        for hh in range(FOX_HEADS):
            own = (lane - d) % 4 == hh
            qa_ref[0, hh, rows, :] = jnp.where(lane < d, q_ref[0, hh, rows, :], qa)
            ka_ref[0, hh, rows, :] = jnp.where(lane < d, k_ref[0, hh, rows, :],
                                               jnp.where(own, ka, 0.0).astype(BF16))


def _fox_bias(x, q, k, blk=256):
    B, S, W = x.shape
    H = q.shape[1]
    return pl.pallas_call(
        functools.partial(_fox_bias_kernel, blk=blk),
        grid=(B,),
        in_specs=[pl.BlockSpec((1, S, W), lambda b: (b, 0, 0)),
                  pl.BlockSpec((1, H, S, LANES), lambda b: (b, 0, 0, 0)),
                  pl.BlockSpec((1, H, S, LANES), lambda b: (b, 0, 0, 0))],
        out_specs=[pl.BlockSpec((1, H, S, LANES), lambda b: (b, 0, 0, 0)),
                   pl.BlockSpec((1, H, S, LANES), lambda b: (b, 0, 0, 0))],
        out_shape=[jax.ShapeDtypeStruct((B, H, S, LANES), BF16), jax.ShapeDtypeStruct((B, H, S, LANES), BF16)],
        compiler_params=_params(("parallel",)),
        name="fox_bias",
    )(x, q, k)


def _tri_masks(tq):
    r = lax.broadcasted_iota(jnp.int32, (tq, tq), 0)
    c = lax.broadcasted_iota(jnp.int32, (tq, tq), 1)
    keep = jnp.where(r >= c, 1.0, 0.0)
    return keep, (1.0 - keep) * NEG_INF


def _causal_attn_kernel(q_ref, k_ref, v_ref, o_ref, s_ref, p_ref, acc_ref, *, tq):
    S = q_ref.shape[2]
    dv = v_ref.shape[-1] // 2
    n = S // tq
    keep, bias_tri = _tri_masks(tq)
    keep_b = keep.astype(BF16)
    outs = []
    for hh in range(q_ref.shape[1]):
        for c in range(n):
            lo, hi = c * tq, (c + 1) * tq
            s = _dot_nt(q_ref[0, hh, lo:, :], k_ref[0, hh, lo:hi, :])
            s_ref[lo:hi, lo:hi] = s[:tq] + bias_tri
            if c < n - 1:
                s_ref[hi:, lo:hi] = s[tq:]
        for i in range(n):
            lo, hi = i * tq, (i + 1) * tq
            s_d = s_ref[lo:hi, lo:hi]
            m = jnp.max(s_d, axis=-1, keepdims=True)
            if i > 0:
                s_o = s_ref[lo:hi, :lo]
                m = jnp.maximum(m, jnp.max(s_o, axis=-1, keepdims=True))
                p_ref[lo:hi, :lo] = jnp.exp2((s_o - m).astype(BF16))
            p_ref[lo:hi, lo:hi] = jnp.exp2((s_d - m).astype(BF16)) * keep_b
        for c in range(n):
            lo, hi = c * tq, (c + 1) * tq
            pv = _dot(p_ref[lo:, lo:hi], v_ref[0, hh, lo:hi, :])
            if c == 0:
                acc_ref[...] = pv
            else:
                acc_ref[lo:, :] += pv
        outs.append(acc_ref[:, :dv] / jnp.maximum(acc_ref[:, dv:dv + 1], 1e-30))
    o_ref[0] = jnp.concatenate(outs, axis=-1)


def _causal_attn(q, k, v, name, tq=256, hpb=2):
    B, H, S, dk = q.shape
    dv = v.shape[-1] // 2

    def heads(w):
        return pl.BlockSpec((1, hpb, S, w), lambda b, h: (b, h, 0, 0))

    return pl.pallas_call(
        functools.partial(_causal_attn_kernel, tq=tq),
        grid=(B, H // hpb),
        in_specs=[heads(dk), heads(dk), heads(2 * dv)],
        out_specs=pl.BlockSpec((1, S, hpb * dv), lambda b, h: (b, 0, h)),
        out_shape=jax.ShapeDtypeStruct((B, S, H * dv), F32),
        scratch_shapes=[pltpu.VMEM((S, S), F32), pltpu.VMEM((S, S), BF16), pltpu.VMEM((S, 2 * dv), F32)],
        compiler_params=_params(("parallel", "parallel")),
        name=name,
    )(q, k, v)


def _nsa_compress_kernel(t_ref, pos_ref, w1_ref, b1_ref, w2_ref, o_ref):
    st = NSA_CMP_STRIDE
    nrow = t_ref.shape[1] // st
    a = [jnp.zeros((nrow, NSA_CMP_HIDDEN), F32)] * 2
    b = [jnp.zeros((nrow, NSA_CMP_HIDDEN), F32)] * 2
    for l in range(st):
        rows = t_ref[0, pl.ds(l, nrow, stride=st), :]
        xa = (rows + pos_ref[l:l + 1, :]).astype(BF16)
        xb = (rows + pos_ref[st + l:st + l + 1, :]).astype(BF16)
        for j in range(2):
            a[j] = a[j] + _dot(xa, w1_ref[j, l * LANES:(l + 1) * LANES, :])
            b[j] = b[j] + _dot(xb, w1_ref[j, (st + l) * LANES:(st + l + 1) * LANES, :])
    for j in range(2):
        hid = jax.nn.gelu(a[j] + pltpu.roll(b[j], nrow - 1, 0) + b1_ref[j])
        o_ref[j, 0] = _dot(hid.astype(BF16), w2_ref[j]).astype(BF16)


def _nsa_compress(t_cmp, pos, w1, b1, w2, layer):
    B, S, _ = t_cmp.shape
    R = S // NSA_CMP_STRIDE
    d = HEAD_DIM

    def par(*shape):
        return pl.BlockSpec((None, 2) + shape, lambda b: (layer, 0, 0, 0))

    return pl.pallas_call(
        _nsa_compress_kernel,
        grid=(B,),
        in_specs=[
            pl.BlockSpec((1, S, LANES), lambda b: (b, 0, 0)),
            pl.BlockSpec((None, NSA_CMP_LEN, LANES), lambda b: (layer, 0, 0)),
            par(NSA_CMP_LEN * LANES, NSA_CMP_HIDDEN), par(1, NSA_CMP_HIDDEN), par(NSA_CMP_HIDDEN, d),
        ],
        out_specs=pl.BlockSpec((2, 1, R, d), lambda b: (0, b, 0, 0)),
        out_shape=jax.ShapeDtypeStruct((2, B, R, d), BF16),
        compiler_params=_params(("parallel",)),
        name="nsa_compress",
    )(t_cmp, pos, w1, b1, w2)


def _nsa_attn_kernel(q_ref, cmp_ref, ksl_ref, vsl_ref, kw_ref, vw_ref, gate_ref, ovl_ref, exp_ref, o_ref, slc_ref, *, tq):
    H = NSA_HEADS
    qi = pl.program_id(1)
    lo = qi * tq
    q4 = q_ref[0].reshape(H * tq, HEAD_DIM)
    pos_q = lo + lax.broadcasted_iota(jnp.int32, (tq, 1), 0)
    lane = lax.broadcasted_iota(jnp.int32, (1, LANES), 1)

    kc, vc = cmp_ref[0, 0], cmp_ref[1, 0]
    nc = kc.shape[0]
    cmp_end = lax.broadcasted_iota(jnp.int32, (1, nc), 1) * NSA_CMP_STRIDE + (NSA_CMP_LEN - 1)
    keep_c = jnp.where(pos_q >= cmp_end, 1.0, 0.0)
    p_c = _masked_softmax2(_dot_nt(q4, kc).reshape(H, tq, nc), keep_c)
    o_cmp = _dot(p_c.reshape(H * tq, nc).astype(BF16), vc).reshape(H, tq, HEAD_DIM)

    nw = NSA_WINDOW + tq
    start = pl.multiple_of(jnp.maximum(lo - NSA_WINDOW, 0), tq)
    kw = kw_ref[0, pl.ds(start, nw), :]
    vw = vw_ref[0, pl.ds(start, nw), :]
    kpos = start + lax.broadcasted_iota(jnp.int32, (1, nw), 1)
    keep_w = jnp.where((kpos <= pos_q) & (kpos > pos_q - NSA_WINDOW), 1.0, 0.0)
    s_w = _dot_nt(q4, kw).reshape(H, tq, nw) + ((1.0 - keep_w) * NEG_INF)[None]
    p_w = jnp.exp2((s_w - jnp.max(s_w, axis=-1, keepdims=True)).astype(BF16)) * keep_w.astype(BF16)[None]
    o_swa = _dot(p_w.reshape(H * tq, nw), vw).reshape(H, tq, 2 * HEAD_DIM)
    o_swa = o_swa[..., :HEAD_DIM] / jnp.maximum(o_swa[..., HEAD_DIM:HEAD_DIM + 1], 1e-30)

    p_sum = p_c[0] + p_c[1] + p_c[2] + p_c[3]
    hi, mid, lo3 = _split3(p_sum)
    ovl = ovl_ref[...]
    imp = _dot(hi, ovl) + _dot(mid, ovl) + _dot(lo3, ovl)
    cur = pos_q // NSA_SEL_LEN
    forced = (lane == 0) | (lane == cur) | (lane == cur - 1)
    score = jnp.where(forced, NSA_FORCE_SCORE, imp)
    score = jnp.where(lane <= cur, score, NEG_INF)
    n_blk = exp_ref.shape[1] // NSA_SEL_LEN
    score_t = score.T[:n_blk]
    blk = lax.broadcasted_iota(jnp.int32, (n_blk, 1), 0)
    cnt = jnp.zeros(score_t.shape, F32)
    for kk in range(n_blk):
        row = score_t[kk:kk + 1, :]
        beats = (row > score_t) | ((row == score_t) & (blk > kk))
        cnt = cnt + jnp.where(beats, 1.0, 0.0)
    sel_t = jnp.where(cnt < min(NSA_SEL_BLOCKS, n_blk), 1.0, 0.0)
    sel = jnp.concatenate([sel_t, jnp.zeros((LANES - n_blk, tq), F32)], axis=0).T.astype(BF16)

    keep_tri, _ = _tri_masks(tq)

    def slc_variant(i):
        lo_k, hi_k = i * tq, (i + 1) * tq
        keep = _dot(sel, exp_ref[:, :hi_k])
        m = jnp.full((H, tq, 1), NEG_INF, F32)
        acc = jnp.zeros((H, tq, 2 * HEAD_DIM), F32)
        for c in range(i + 1):
            ks = slice(c * tq, (c + 1) * tq)
            keep_c = keep[:, ks] * keep_tri if c == i else keep[:, ks]
            s = _dot_nt(q4, ksl_ref[0, ks, :]).reshape(H, tq, tq) + ((1.0 - keep_c) * NEG_INF)[None]
            m_new = jnp.maximum(m, jnp.max(s, axis=-1, keepdims=True))
            p = jnp.exp2((s - m_new).astype(BF16)) * keep_c.astype(BF16)[None]
            pv = _dot(p.reshape(H * tq, tq), vsl_ref[0, ks, :]).reshape(H, tq, 2 * HEAD_DIM)
            acc = jnp.exp2(m - m_new) * acc + pv
            m = m_new
        slc_ref[...] = acc[..., :HEAD_DIM] / jnp.maximum(acc[..., HEAD_DIM:HEAD_DIM + 1], 1e-30)

    for i in range(ksl_ref.shape[1] // tq):
        pl.when(qi == i)(functools.partial(slc_variant, i))
    o_slc = slc_ref[...]

    g = gate_ref[0]
    o_ref[0] = jnp.concatenate(
        [g[:, hh:hh + 1] * o_cmp[hh] + g[:, H + hh:H + hh + 1] * o_slc[hh] + g[:, 2 * H + hh:2 * H + hh + 1] * o_swa[hh]
         for hh in range(H)], axis=-1)


def _nsa_attn(q, cmp, ksl, vsl, kw, vw, gates, ovl, expand, tq=256):
    B, H, S, d = q.shape
    R = cmp.shape[2]

    def full(w):
        return pl.BlockSpec((1, S, w), lambda b, i: (b, 0, 0))

    return pl.pallas_call(
        functools.partial(_nsa_attn_kernel, tq=tq),
        grid=(B, S // tq),
        in_specs=[
            pl.BlockSpec((1, H, tq, d), lambda b, i: (b, 0, i, 0)),
            pl.BlockSpec((2, 1, R, d), lambda b, i: (0, b, 0, 0)),
            full(d), full(2 * d), full(d), full(2 * d),
            pl.BlockSpec((1, tq, LANES), lambda b, i: (b, i, 0)),
            pl.BlockSpec((R, LANES), lambda b, i: (0, 0)),
            pl.BlockSpec((LANES, S), lambda b, i: (0, 0)),
        ],
        out_specs=pl.BlockSpec((1, tq, H * d), lambda b, i: (b, i, 0)),
        out_shape=jax.ShapeDtypeStruct((B, S, H * d), F32),
        scratch_shapes=[pltpu.VMEM((H, tq, d), F32)],
        compiler_params=_params(("parallel", "arbitrary")),
        name="nsa_attn",
    )(q, cmp, ksl, vsl, kw, vw, gates, ovl, expand)


def _s5_kernel(u_ref, bre_ref, bim_ref, kr_ref, ki_ref, ar_ref, ai_ref, cre_ref, cim_ref, d_ref, y_ref,
               xr_ref, xi_ref, sr_ref, si_ref, us_ref, *, tc, nb, lane_blk):
    @pl.when(pl.program_id(0) == 0)
    def _():
        sr_ref[...] = jnp.zeros(sr_ref.shape, F32)
        si_ref[...] = jnp.zeros(si_ref.shape, F32)

    W = S5_WIDTH
    ncol = W // LANES
    for b in range(nb):
        for c in range(ncol):
            us_ref[c, pl.ds(b, tc, stride=nb), :] = u_ref[:, b * W + c * LANES:b * W + (c + 1) * LANES]
    u = jnp.concatenate([us_ref[c] for c in range(ncol)], axis=-1)
    ub = u.astype(BF16)
    kr, ki = kr_ref[...], ki_ref[...]
    bre, bim = bre_ref[...], bim_ref[...]
    xr_ref[...] = _dot(ub, (bre * kr - bim * ki).astype(BF16))
    xi_ref[...] = _dot(ub, (bre * ki + bim * kr).astype(BF16))

    for c in range(S5_STATES // lane_blk):
        sl = slice(c * lane_blk, (c + 1) * lane_blk)
        ar = jnp.broadcast_to(ar_ref[:, sl], (nb, lane_blk))
        ai = jnp.broadcast_to(ai_ref[:, sl], (nb, lane_blk))

        def step(t, carry):
            sr, si = carry
            row = pl.ds(pl.multiple_of(t * nb, nb), nb)
            nr = ar * sr - ai * si + xr_ref[row, sl]
            ni = ar * si + ai * sr + xi_ref[row, sl]
            xr_ref[row, sl] = nr
            xi_ref[row, sl] = ni
            return nr, ni

        sr, si = lax.fori_loop(0, tc, step, (sr_ref[:, sl], si_ref[:, sl]), unroll=4)
        sr_ref[:, sl] = sr
        si_ref[:, sl] = si

    y = (_dot(xr_ref[...].astype(BF16), cre_ref[...]) - _dot(xi_ref[...].astype(BF16), cim_ref[...])
         + d_ref[...] * u)
    for c in range(ncol):
        us_ref[c] = y[:, c * LANES:(c + 1) * LANES]
    for b in range(nb):
        for c in range(ncol):
            y_ref[:, b * W + c * LANES:b * W + (c + 1) * LANES] = us_ref[c, pl.ds(b, tc, stride=nb), :]


def _s5_scan(u_t, bre, bim, kr, ki, ar, ai, cre, cim, d, tc=64, lane_blk=512):
    S = u_t.shape[0]
    W = S5_WIDTH
    B = u_t.shape[1] // W
    const = lambda shape: pl.BlockSpec(shape, lambda i: (0, 0))
    return pl.pallas_call(
        functools.partial(_s5_kernel, tc=tc, nb=B, lane_blk=lane_blk),
        grid=(S // tc,),
        in_specs=[
            pl.BlockSpec((tc, B * W), lambda i: (i, 0)),
            const((W, S5_STATES)), const((W, S5_STATES)),
            const((1, S5_STATES)), const((1, S5_STATES)), const((1, S5_STATES)), const((1, S5_STATES)),
            const((S5_STATES, W)), const((S5_STATES, W)), const((1, W)),
        ],
        out_specs=pl.BlockSpec((tc, B * W), lambda i: (i, 0)),
        out_shape=jax.ShapeDtypeStruct((S, B * W), F32),
        scratch_shapes=[
            pltpu.VMEM((tc * B, S5_STATES), F32), pltpu.VMEM((tc * B, S5_STATES), F32),
            pltpu.VMEM((B, S5_STATES), F32), pltpu.VMEM((B, S5_STATES), F32),
            pltpu.VMEM((W // LANES, tc * B, LANES), F32),
        ],
        compiler_params=_params(("arbitrary",)),
        name="s5_scan",
    )(u_t, bre, bim, kr, ki, ar, ai, cre, cim, d)


def _mix_out_kernel(h_ref, om_ref, on_ref, of_ref, ys_ref, gw_ref, gb_ref, gg_ref, wo_ref, pg_ref, o_ref):
    gg = gg_ref[0]
    y = jax.nn.gelu(ys_ref[...])
    y = y * jax.nn.sigmoid(_dot(y.astype(BF16), gw_ref[0]) + gb_ref[...])
    groups = (om_ref[0], on_ref[0], of_ref[0], y)
    cat = jnp.concatenate([_rms(p, gg[i:i + 1, :]).astype(BF16) for i, p in enumerate(groups)], axis=-1)
    o_ref[0] = h_ref[0] + _rms(_dot(cat, wo_ref[0]), pg_ref[...])


def _mix_out(h, o_mla, o_nsa, o_fox, y_s5, glu_w, glu_b, group_g, w_out, post_g, layer, tm=1024):
    B, S, D = h.shape

    def group():
        return pl.BlockSpec((1, tm, GROUP_WIDTH), lambda b, t: (b, t, 0))

    return pl.pallas_call(
        _mix_out_kernel,
        grid=(B, S // tm),
        in_specs=[
            pl.BlockSpec((1, tm, D), lambda b, t: (b, t, 0)),
            group(), group(), group(),
            pl.BlockSpec((tm, S5_WIDTH), lambda b, t: (t, b)),
            pl.BlockSpec((1, S5_WIDTH, S5_WIDTH), lambda b, t: (layer, 0, 0)),
            pl.BlockSpec((None, 1, S5_WIDTH), lambda b, t: (layer, 0, 0)),
            pl.BlockSpec((1, 4, GROUP_WIDTH), lambda b, t: (layer, 0, 0)),
            pl.BlockSpec((1, D, D), lambda b, t: (layer, 0, 0)),
            pl.BlockSpec((None, 1, D), lambda b, t: (layer, 0, 0)),
        ],
        out_specs=pl.BlockSpec((1, tm, D), lambda b, t: (b, t, 0)),
        out_shape=jax.ShapeDtypeStruct((B, S, D), F32),
        compiler_params=_params(("parallel", "parallel")),
        name="mix_out",
    )(h, o_mla, o_nsa, o_fox, y_s5, glu_w, glu_b, group_g, w_out, post_g)


def _xa_kv_kernel(mem_ref, g_ref, wkv_ref, kv_ref):
    m = _rms(mem_ref[0], g_ref[...]).astype(BF16)
    kv_ref[0] = _dot(m, wkv_ref[0]).astype(BF16)


def _xa_kv(mem, g, wkv, layer):
    B, M, D = mem.shape
    return pl.pallas_call(
        _xa_kv_kernel,
        grid=(B,),
        in_specs=[
            pl.BlockSpec((1, M, D), lambda b: (b, 0, 0)),
            pl.BlockSpec((None, 1, D), lambda b: (layer, 0, 0)),
            pl.BlockSpec((1, D, 2 * D), lambda b: (layer, 0, 0)),
        ],
        out_specs=pl.BlockSpec((1, M, 2 * D), lambda b: (b, 0, 0)),
        out_shape=jax.ShapeDtypeStruct((B, M, 2 * D), BF16),
        compiler_params=_params(("parallel",)),
        name="xa_kv",
    )(mem, g, wkv)


def _xa_kernel(h_ref, kv_ref, pre_g_ref, wq_ref, wo_ref, post_g_ref, o_ref):
    h = h_ref[0]
    u = _rms(h, pre_g_ref[...]).astype(BF16)
    q = _dot(u, wq_ref[0]) * (XA_HEAD_DIM ** -0.5)
    acc = jnp.zeros(h.shape, F32)
    for hh in range(XA_HEADS):
        sl = slice(hh * XA_HEAD_DIM, (hh + 1) * XA_HEAD_DIM)
        k = kv_ref[0, :, sl]
        v = kv_ref[0, :, D_MODEL + hh * XA_HEAD_DIM:D_MODEL + (hh + 1) * XA_HEAD_DIM]
        s = _dot_nt(q[:, sl].astype(BF16), k)
        m = jnp.max(s, axis=-1, keepdims=True)
        e = jnp.exp(s - m)
        p = e / jnp.sum(e, axis=-1, keepdims=True)
        o = _dot(p.astype(BF16), v)
        acc = acc + _dot(o.astype(BF16), wo_ref[0, sl, :])
    o_ref[0] = h + _rms(acc, post_g_ref[...])


def _xa(h, kv, pre_g, wq, wo, post_g, layer, tm=1024):
    B, S, D = h.shape
    M = kv.shape[1]
    return pl.pallas_call(
        _xa_kernel,
        grid=(B, S // tm),
        in_specs=[
            pl.BlockSpec((1, tm, D), lambda b, t: (b, t, 0)),
            pl.BlockSpec((1, M, 2 * D), lambda b, t: (b, 0, 0)),
            pl.BlockSpec((None, 1, D), lambda b, t: (layer, 0, 0)),
            pl.BlockSpec((1, D, D), lambda b, t: (layer, 0, 0)),
            pl.BlockSpec((1, D, D), lambda b, t: (layer, 0, 0)),
            pl.BlockSpec((None, 1, D), lambda b, t: (layer, 0, 0)),
        ],
        out_specs=pl.BlockSpec((1, tm, D), lambda b, t: (b, t, 0)),
        out_shape=jax.ShapeDtypeStruct((B, S, D), F32),
        compiler_params=_params(("parallel", "parallel")),
        name="cross_attn",
    )(h, kv, pre_g, wq, wo, post_g)


def _rope_tables(n, dim):
    inv = 1.0 / (ROPE_THETA ** (jnp.arange(0, dim, 2, dtype=F32) / dim))
    ang = jnp.arange(n, dtype=F32)[:, None] * inv[None, :]
    return jnp.cos(ang), jnp.sin(ang)


def _pack_w_in(w):
    L, D, _ = w.shape
    z = lambda n: jnp.zeros((L, D, n), w.dtype)
    o = 0
    cuts = {}
    for name, n in (("cq", MLA_Q_LORA), ("ckv", MLA_KV_LORA), ("kpe", MLA_ROPE_DIM), ("nq", 256), ("kc", 64),
                    ("vc", 64), ("ksl", 64), ("vsl", 64), ("kw", 64), ("vw", 64), ("gn", N_GATES),
                    ("fq", 256), ("fk", 256), ("fv", 256), ("ff", FOX_HEADS), ("s5", S5_WIDTH)):
        cuts[name] = w[:, :, o:o + n]
        o += n
    c = cuts
    return jnp.concatenate([
        c["cq"], z(64), c["ckv"], z(MLA_NOPE_DIM), c["kpe"], z(32), c["nq"], c["kc"], c["vc"], c["ksl"], c["vsl"],
        c["kw"], c["vw"], c["gn"], c["ff"], z(LANES - N_GATES - FOX_HEADS), c["fq"], c["fk"], c["fv"], c["s5"],
    ], axis=-1)


def kernel(x, mem, ffn1_pre_g, ffn1_w_gate, ffn1_w_up, ffn1_w_down, ffn1_post_g, mix_pre_g, mix_w_in, mla_q_norm_g, mla_w_uq, mla_kv_norm_g, mla_w_ukv, nsa_cmp_pos, nsa_phi_w1, nsa_phi_b1, nsa_phi_w2, nsa_gate_b, fox_f_b, s5_lambda_re, s5_lambda_im, s5_log_dt, s5_b_re, s5_b_im, s5_c_re, s5_c_im, s5_d, s5_glu_w, s5_glu_b, mix_group_g, mix_w_out, mix_post_g, xa_pre_g, xa_mem_g, xa_w_q, xa_w_kv, xa_w_o, xa_post_g, ffn2_pre_g, ffn2_w_gate, ffn2_w_up, ffn2_w_down, ffn2_post_g):
    B, S, D = x.shape
    L = ffn1_pre_g.shape[0]
    T = B * S
    G, P, C = S5_GROUPS, S5_STATE, S5_GROUP
    bf = lambda a: a.astype(BF16)

    w_in = bf(_pack_w_in(mix_w_in))
    qg = jnp.pad(mla_q_norm_g, ((0, 0), (0, 256 - MLA_Q_LORA)))
    wuq = mla_w_uq.reshape(L, MLA_Q_LORA, MLA_HEADS, MLA_NOPE_DIM + MLA_ROPE_DIM)
    wuq = jnp.pad(wuq, ((0, 0), (0, 256 - MLA_Q_LORA), (0, 0), (0, MLA_PAD_DIM - MLA_NOPE_DIM - MLA_ROPE_DIM)))
    wuq = bf(wuq.reshape(L, 256, MLA_HEADS * MLA_PAD_DIM))
    wukv = mla_w_ukv.reshape(L, MLA_KV_LORA, MLA_HEADS, MLA_NOPE_DIM + MLA_V_DIM)
    wk = jnp.pad(wukv[..., :MLA_NOPE_DIM], ((0, 0), (0, 0), (0, 0), (0, MLA_PAD_DIM - MLA_NOPE_DIM)))
    wukv = bf(jnp.concatenate([wk.reshape(L, MLA_KV_LORA, -1), wukv[..., MLA_NOPE_DIM:].reshape(L, MLA_KV_LORA, -1)], axis=-1))
    misc_bias = jnp.concatenate([nsa_gate_b, fox_f_b, jnp.zeros((L, LANES - N_GATES - FOX_HEADS), F32)], axis=-1)

    c32, s32 = _rope_tables(S, HEAD_DIM)
    cos64 = jnp.tile(jnp.concatenate([c32, c32], axis=-1), (1, 2))
    sin64 = jnp.tile(jnp.concatenate([-s32, s32], axis=-1), (1, 2))
    c16, s16 = _rope_tables(S, MLA_ROPE_DIM)
    one, zero = jnp.ones((S, 1), F32), jnp.zeros((S, 1), F32)
    cosm = jnp.concatenate([jnp.tile(one, (1, 64)), c16, c16, jnp.tile(one, (1, 32))], axis=-1)
    sinm = jnp.concatenate([jnp.tile(zero, (1, 64)), -s16, s16, jnp.tile(zero, (1, 32))], axis=-1)

    cmp_pos = jnp.concatenate([nsa_cmp_pos[:, 0], nsa_cmp_pos[:, 1]], axis=-1)
    w1 = nsa_phi_w1.reshape(L, 2, NSA_CMP_LEN, HEAD_DIM, NSA_CMP_HIDDEN)
    w1 = jnp.stack([jnp.pad(w1[:, 0], ((0, 0), (0, 0), (0, HEAD_DIM), (0, 0))),
                    jnp.pad(w1[:, 1], ((0, 0), (0, 0), (HEAD_DIM, 0), (0, 0)))], axis=1)
    phi_w1 = bf(w1.reshape(L, 2, NSA_CMP_LEN * LANES, NSA_CMP_HIDDEN))
    phi_w2 = bf(nsa_phi_w2)
    phi_b1 = nsa_phi_b1.reshape(L, 2, 1, NSA_CMP_HIDDEN)

    n_cmp_pad = S // NSA_CMP_STRIDE
    n_blk = S // NSA_SEL_LEN
    cs = np.arange(n_cmp_pad) * NSA_CMP_STRIDE
    ss = np.arange(LANES) * NSA_SEL_LEN
    ovl_np = ((cs[:, None] < ss[None, :] + NSA_SEL_LEN) & (cs[:, None] + NSA_CMP_LEN > ss[None, :])
              & (np.arange(LANES)[None, :] < n_blk) & (np.arange(n_cmp_pad)[:, None] < n_cmp_pad - 1))
    ovl = jnp.asarray(ovl_np.astype(np.float32), dtype=BF16)
    tq_nsa = 256
    key_blk = np.arange(S) // NSA_SEL_LEN
    expand = jnp.asarray((key_blk[None, :] == np.arange(LANES)[:, None]).astype(np.float32), dtype=BF16)

    dt = jnp.exp(s5_log_dt)[:, :, None]
    lr, li = s5_lambda_re, s5_lambda_im
    mag = jnp.exp(lr * dt)
    a_r, a_i = mag * jnp.cos(li * dt), mag * jnp.sin(li * dt)
    den = lr * lr + li * li
    k_r = ((a_r - 1.0) * lr + a_i * li) / den
    k_i = (a_i * lr - (a_r - 1.0) * li) / den
    flat = lambda a: a.reshape(L, 1, G * P)
    a_r, a_i, k_r, k_i = flat(a_r), flat(a_i), flat(k_r), flat(k_i)
    eye = jnp.eye(G, dtype=F32)
    blk_b = lambda b: jnp.einsum('lgpc,gh->lgchp', b, eye).reshape(L, G * C, G * P)
    blk_c = lambda c: jnp.einsum('lgcp,gh->lgphc', c, eye).reshape(L, G * P, G * C)
    b_re, b_im = blk_b(s5_b_re), blk_b(s5_b_im)
    c_re, c_im = bf(blk_c(s5_c_re)), bf(blk_c(s5_c_im))
    d_skip = s5_d.reshape(L, 1, S5_WIDTH)

    ffn_w = [(bf(ffn1_w_gate), bf(ffn1_w_up), bf(ffn1_w_down)), (bf(ffn2_w_gate), bf(ffn2_w_up), bf(ffn2_w_down))]
    glu_w, w_out = bf(s5_glu_w), bf(mix_w_out)
    wq_x, wkv_x, wo_x = bf(xa_w_q), bf(xa_w_kv), bf(xa_w_o)

    v3 = lambda a: a.reshape(L, 1, a.shape[-1])
    ffn1_pre_g, ffn1_post_g, ffn2_pre_g, ffn2_post_g = v3(ffn1_pre_g), v3(ffn1_post_g), v3(ffn2_pre_g), v3(ffn2_post_g)
    mix_pre_g, mix_post_g, qg, kvg, misc_bias = v3(mix_pre_g), v3(mix_post_g), v3(qg), v3(mla_kv_norm_g), v3(misc_bias)
    s5_glu_b, xa_mem_g, xa_pre_g, xa_post_g = v3(s5_glu_b), v3(xa_mem_g), v3(xa_pre_g), v3(xa_post_g)

    h = x
    for l in range(L):
        h = _ffn(h.reshape(T, D), ffn1_pre_g, *ffn_w[0], ffn1_post_g, l).reshape(B, S, D)

        (mq, mk, mv, nq, ncmp, nksl, nvsl, nkw, nvw, misc, fq, fk, fv, s5u) = _proj(
            h, mix_pre_g, w_in, qg, wuq, kvg, wukv, misc_bias, cos64, sin64, cosm, sinm, l)

        o_mla = _causal_attn(mq, mk, mv, "mla_attn")

        fq_b, fk_b = _fox_bias(misc, fq, fk)
        o_fox = _causal_attn(fq_b, fk_b, fv, "fox_attn")

        cmp = _nsa_compress(ncmp, cmp_pos, phi_w1, phi_b1, phi_w2, l)
        o_nsa = _nsa_attn(nq, cmp, nksl, nvsl, nkw, nvw, misc, ovl, expand, tq=tq_nsa)

        y_s5 = _s5_scan(s5u, b_re[l], b_im[l], k_r[l], k_i[l], a_r[l], a_i[l],
                        c_re[l], c_im[l], d_skip[l])

        h = _mix_out(h, o_mla, o_nsa, o_fox, y_s5, glu_w, s5_glu_b, mix_group_g, w_out, mix_post_g, l)

        kv = _xa_kv(mem, xa_mem_g, wkv_x, l)
        h = _xa(h, kv, xa_pre_g, wq_x, wo_x, xa_post_g, l)

        h = _ffn(h.reshape(T, D), ffn2_pre_g, *ffn_w[1], ffn2_post_g, l).reshape(B, S, D)
    return h
```

```python
import functools
import math

import numpy as np
import jax
import jax.numpy as jnp
from jax import lax
from jax.experimental import pallas as pl
from jax.experimental.pallas import tpu as pltpu

F32 = jnp.float32
BF16 = jnp.bfloat16

D_MODEL = 1024
HEAD_DIM = 64
ROPE_THETA = 10000.0
NORM_EPS = 1e-6
NEG_INF = -1e30
LOG2E = math.log2(math.e)
MACARON_WEIGHT = 0.5
FFN_DIM = 2816
XA_HEADS = 4
XA_HEAD_DIM = D_MODEL // XA_HEADS

MLA_HEADS = 4
MLA_Q_LORA = 192
MLA_KV_LORA = 128
MLA_NOPE_DIM = 64
MLA_ROPE_DIM = 32
MLA_V_DIM = 64
MLA_PAD_DIM = 128

NSA_HEADS = 4
NSA_CMP_LEN = 32
NSA_CMP_STRIDE = 16
NSA_CMP_HIDDEN = 256
NSA_SEL_LEN = 64
NSA_SEL_BLOCKS = 8
NSA_WINDOW = 256
NSA_FORCE_SCORE = 1e4

FOX_HEADS = 4

S5_WIDTH = 256
S5_GROUP = 16
S5_GROUPS = S5_WIDTH // S5_GROUP
S5_STATE = 64
S5_STATES = S5_GROUPS * S5_STATE

GROUP_WIDTH = 256
LANES = 128
CUMSUM_BLOCK = 256
VMEM_LIMIT = 56 * 1024 * 1024

SEG_CQ, SEG_CKV, SEG_KPE, SEG_NQ = 0, 256, 384, 512
SEG_NCMP, SEG_NSLC, SEG_NSWA, SEG_MISC = 768, 896, 1024, 1152
SEG_FQ, SEG_FK, SEG_FV, SEG_S5, IN_PACKED = 1280, 1536, 1792, 2048, 2304
N_GATES = 3 * NSA_HEADS


def _params(sem):
    return pltpu.CompilerParams(dimension_semantics=sem, vmem_limit_bytes=VMEM_LIMIT)


def _rms(x, g, n=None):
    n = x.shape[-1] if n is None else n
    ms = jnp.sum(x * x, axis=-1, keepdims=True) * (1.0 / n)
    return x * lax.rsqrt(ms + NORM_EPS) * g


def _dot(a, b):
    return jnp.dot(a, b, preferred_element_type=F32)


def _dot_nt(a, b):
    return lax.dot_general(a, b, (((1,), (1,)), ((), ())), preferred_element_type=F32)


def _split3(x):
    hi = x.astype(BF16)
    r = x - hi.astype(F32)
    mid = r.astype(BF16)
    lo = (r - mid.astype(F32)).astype(BF16)
    return hi, mid, lo


def _masked_softmax2(s, keep):
    s = s + ((1.0 - keep) * NEG_INF)[None]
    m = jnp.max(s, axis=-1, keepdims=True)
    e = jnp.exp2(s - m) * keep[None]
    return e / jnp.maximum(jnp.sum(e, axis=-1, keepdims=True), 1e-30)


def _ffn_kernel(h_ref, pre_g_ref, wg_ref, wu_ref, wd_ref, post_g_ref, o_ref, *, chunk):
    h = h_ref[...]
    u = _rms(h, pre_g_ref[...]).astype(BF16)
    acc = jnp.zeros(h.shape, F32)
    for c in range(wg_ref.shape[1] // chunk):
        sl = slice(c * chunk, (c + 1) * chunk)
        g = _dot(u, wg_ref[:, sl])
        p = _dot(u, wu_ref[:, sl])
        a = (g * jax.nn.sigmoid(g)) * p
        acc = acc + _dot(a.astype(BF16), wd_ref[sl, :])
    o_ref[...] = h + MACARON_WEIGHT * _rms(acc, post_g_ref[...])


def _ffn(h, pre_g, wg, wu, wd, post_g, layer, tm=1024, chunk=256):
    T, D = h.shape
    F = wg.shape[-1]
    wspec = functools.partial(pl.BlockSpec, pipeline_mode=pl.Buffered(1))
    return pl.pallas_call(
        functools.partial(_ffn_kernel, chunk=chunk),
        grid=(T // tm,),
        in_specs=[
            pl.BlockSpec((tm, D), lambda i: (i, 0)),
            pl.BlockSpec((None, 1, D), lambda i: (layer, 0, 0)),
            wspec((None, D, F), lambda i: (layer, 0, 0)),
            wspec((None, D, F), lambda i: (layer, 0, 0)),
            wspec((None, F, D), lambda i: (layer, 0, 0)),
            pl.BlockSpec((None, 1, D), lambda i: (layer, 0, 0)),
        ],
        out_specs=pl.BlockSpec((tm, D), lambda i: (i, 0)),
        out_shape=jax.ShapeDtypeStruct((T, D), F32),
        compiler_params=_params(("parallel",)),
        name="ffn",
    )(h, pre_g, wg, wu, wd, post_g)


def _proj_kernel(h_ref, g_ref, win_ref, qg_ref, wuq_ref, kvg_ref, wukv_ref, bias_ref,
                 cos64_ref, sin64_ref, cosm_ref, sinm_ref,
                 mq_ref, mk_ref, mv_ref, nq_ref, ncmp_ref, nksl_ref, nvsl_ref, nkw_ref, nvw_ref,
                 misc_ref, fq_ref, fk_ref, fv_ref, s5_ref, carry_ref):
    h = h_ref[0]
    tm = h.shape[0]
    u = _rms(h, g_ref[...]).astype(BF16)
    z = _dot(u, win_ref[...])
    lane = lax.broadcasted_iota(jnp.int32, (tm, LANES), 1)

    def rope(x, c, s, half):
        fwd = pltpu.roll(x, LANES - half, 1)
        bwd = pltpu.roll(x, half, 1)
        return x * c + jnp.where(lane % (2 * half) < half, fwd, bwd) * s

    def padded(x, j, fill):
        blk = x[:, (j // 2) * LANES:(j // 2 + 1) * LANES]
        if j % 2:
            blk = pltpu.roll(blk, HEAD_DIM, 1)
        return jnp.where(lane < HEAD_DIM, blk, fill).astype(BF16)

    def with_ones(x, j):
        return padded(x, j, 1.0)

    cos64, sin64 = cos64_ref[...], sin64_ref[...]
    cosm, sinm = cosm_ref[...], sinm_ref[...]
    half_m = MLA_ROPE_DIM // 2
    half_n = HEAD_DIM // 2

    cq = _rms(z[:, SEG_CQ:SEG_CKV], qg_ref[...], MLA_Q_LORA).astype(BF16)
    q = _dot(cq, wuq_ref[...])
    ckv = _rms(z[:, SEG_CKV:SEG_KPE], kvg_ref[...]).astype(BF16)
    kv = _dot(ckv, wukv_ref[...])
    kpe = rope(z[:, SEG_KPE:SEG_NQ], cosm, sinm, half_m)
    mla_scale = (MLA_NOPE_DIM + MLA_ROPE_DIM) ** -0.5 * LOG2E
    for hh in range(MLA_HEADS):
        sl = slice(hh * MLA_PAD_DIM, (hh + 1) * MLA_PAD_DIM)
        mq_ref[0, hh] = (rope(q[:, sl], cosm, sinm, half_m) * mla_scale).astype(BF16)
        mk_ref[0, hh] = (kv[:, sl] + kpe).astype(BF16)
        mv_ref[0, hh] = with_ones(kv[:, MLA_HEADS * MLA_PAD_DIM:], hh)

    scale = HEAD_DIM ** -0.5 * LOG2E
    for c in range(NSA_HEADS // 2):
        r = rope(z[:, SEG_NQ + c * LANES:SEG_NQ + (c + 1) * LANES], cos64, sin64, half_n) * scale
        nq_ref[0, 2 * c] = r[:, :HEAD_DIM].astype(BF16)
        nq_ref[0, 2 * c + 1] = r[:, HEAD_DIM:].astype(BF16)
    seg = z[:, SEG_NCMP:SEG_NSLC]
    ncmp_ref[0] = jnp.where(lane < HEAD_DIM, rope(seg, cos64, sin64, half_n), seg)
    seg = z[:, SEG_NSLC:SEG_NSWA]
    nksl_ref[0] = rope(seg, cos64, sin64, half_n)[:, :HEAD_DIM].astype(BF16)
    nvsl_ref[0] = with_ones(seg, 1)
    seg = z[:, SEG_NSWA:SEG_MISC]
    nkw_ref[0] = rope(seg, cos64, sin64, half_n)[:, :HEAD_DIM].astype(BF16)
    nvw_ref[0] = with_ones(seg, 1)

    x = z[:, SEG_MISC:SEG_FQ] + bias_ref[...]
    log_sig = jnp.minimum(x, 0.0) - jnp.log(1.0 + jnp.exp(-jnp.abs(x)))
    misc = jnp.where(lane < N_GATES, jax.nn.sigmoid(x), log_sig * LOG2E)
    misc_ref[0] = misc

    @pl.when(pl.program_id(1) == 0)
    def _():
        carry_ref[...] = jnp.zeros(carry_ref.shape, F32)

    d = HEAD_DIM
    blk = CUMSUM_BLOCK
    r = lax.broadcasted_iota(jnp.int32, (blk, blk), 0)
    c = lax.broadcasted_iota(jnp.int32, (blk, blk), 1)
    tri = jnp.where(r >= c, 1.0, 0.0).astype(BF16)
    lane_b = lax.broadcasted_iota(jnp.int32, (blk, LANES), 1)
    first, second = (lane_b >= d) & (lane_b < d + 12), (lane_b >= d + 12) & (lane_b < d + 24)
    carry = carry_ref[...]
    for i in range(tm // blk):
        rows = slice(i * blk, (i + 1) * blk)
        hi, mid, lo = _split3(misc[rows])
        cs = _dot(tri, hi) + _dot(tri, mid) + _dot(tri, lo) + carry
        carry = cs[blk - 1:blk, :]
        val = jnp.zeros((blk, LANES), F32)
        for j, p in enumerate(_split3(cs)):
            grp = (lane_b >= d + 4 * j) & (lane_b < d + 4 * j + 4)
            val = jnp.where(grp, pltpu.roll(p.astype(F32), d + 4 * j - N_GATES, 1), val)
        qa = jnp.where(second, 1.0, val)
        ka = jnp.where(first, 1.0, jnp.where(second, -pltpu.roll(val, 12, 1), 0.0))
        for hh in range(FOX_HEADS):
            own = (lane_b - d) % 4 == hh
            piece = slice((hh // 2) * LANES, (hh // 2 + 1) * LANES)
            q_h, k_h = z[rows, SEG_FQ:SEG_FK][:, piece] * scale, z[rows, SEG_FK:SEG_FV][:, piece]
            if hh % 2:
                q_h, k_h = pltpu.roll(q_h, d, 1), pltpu.roll(k_h, d, 1)
            fq_ref[0, hh, rows, :] = jnp.where(lane_b < d, q_h, qa).astype(BF16)
            fk_ref[0, hh, rows, :] = jnp.where(lane_b < d, k_h, jnp.where(own, ka, 0.0)).astype(BF16)
    carry_ref[...] = carry
    for hh in range(FOX_HEADS):
        fv_ref[0, hh] = with_ones(z[:, SEG_FV:SEG_S5], hh)

    s5_ref[...] = z[:, SEG_S5:IN_PACKED]


def _proj(h, g, win, qg, wuq, kvg, wukv, bias, cos64, sin64, cosm, sinm, layer, tm=1024):
    B, S, D = h.shape
    nt = S // tm
    wspec = functools.partial(pl.BlockSpec, pipeline_mode=pl.Buffered(1))

    def w2(shape):
        return wspec((None,) + shape, lambda b, t: (layer, 0, 0))

    def vec(n):
        return pl.BlockSpec((None, 1, n), lambda b, t: (layer, 0, 0))

    def tab():
        return pl.BlockSpec((tm, LANES), lambda b, t: (t, 0))

    def heads(d):
        return pl.BlockSpec((1, 4, tm, d), lambda b, t: (b, 0, t, 0))

    def tok(d):
        return pl.BlockSpec((1, tm, d), lambda b, t: (b, t, 0))

    def hshape(d, dt=BF16):
        return jax.ShapeDtypeStruct((B, 4, S, d), dt)

    def tshape(d, dt):
        return jax.ShapeDtypeStruct((B, S, d), dt)

    return pl.pallas_call(
        _proj_kernel,
        grid=(B, nt),
        in_specs=[
            tok(D), vec(D), w2((D, IN_PACKED)), vec(256), w2((256, 4 * MLA_PAD_DIM)),
            vec(MLA_KV_LORA), w2((MLA_KV_LORA, 4 * MLA_PAD_DIM + 4 * MLA_V_DIM)), vec(LANES),
            tab(), tab(), tab(), tab(),
        ],
        out_specs=[
            heads(MLA_PAD_DIM), heads(MLA_PAD_DIM), heads(LANES),
            heads(HEAD_DIM), tok(LANES), tok(HEAD_DIM), tok(LANES), tok(HEAD_DIM), tok(LANES),
            tok(LANES), heads(LANES), heads(LANES), heads(LANES),
            pl.BlockSpec((tm, S5_WIDTH), lambda b, t: (t, b)),
        ],
        out_shape=[
            hshape(MLA_PAD_DIM), hshape(MLA_PAD_DIM), hshape(LANES),
            hshape(HEAD_DIM), tshape(LANES, F32),
            tshape(HEAD_DIM, BF16), tshape(LANES, BF16), tshape(HEAD_DIM, BF16), tshape(LANES, BF16),
            tshape(LANES, F32), hshape(LANES), hshape(LANES), hshape(LANES),
            jax.ShapeDtypeStruct((S, B * S5_WIDTH), F32),
        ],
        scratch_shapes=[pltpu.VMEM((1, LANES), F32)],
        compiler_params=_params(("parallel", "arbitrary")),
        name="mix_proj",
    )(h, g, win, qg, wuq, kvg, wukv, bias, cos64, sin64, cosm, sinm)


def _tri_masks(tq):
    r = lax.broadcasted_iota(jnp.int32, (tq, tq), 0)
    c = lax.broadcasted_iota(jnp.int32, (tq, tq), 1)
    keep = jnp.where(r >= c, 1.0, 0.0)
    return keep, (1.0 - keep) * NEG_INF


def _causal_attn_kernel(q_ref, k_ref, v_ref, o_ref, s_ref, p_ref, acc_ref, *, tq):
    S = q_ref.shape[2]
    dv = v_ref.shape[-1] // 2
    n = S // tq
    keep, bias_tri = _tri_masks(tq)
    keep_b = keep.astype(BF16)
    outs = []
    for hh in range(q_ref.shape[1]):
        for c in range(n):
            lo, hi = c * tq, (c + 1) * tq
            s = _dot_nt(q_ref[0, hh, lo:, :], k_ref[0, hh, lo:hi, :])
            s_ref[lo:hi, lo:hi] = s[:tq] + bias_tri
            if c < n - 1:
                s_ref[hi:, lo:hi] = s[tq:]
        for i in range(n):
            lo, hi = i * tq, (i + 1) * tq
            s_d = s_ref[lo:hi, lo:hi]
            m = jnp.max(s_d, axis=-1, keepdims=True)
            if i > 0:
                s_o = s_ref[lo:hi, :lo]
                m = jnp.maximum(m, jnp.max(s_o, axis=-1, keepdims=True))
                p_ref[lo:hi, :lo] = jnp.exp2((s_o - m).astype(BF16))
            p_ref[lo:hi, lo:hi] = jnp.exp2((s_d - m).astype(BF16)) * keep_b
        for c in range(n):
            lo, hi = c * tq, (c + 1) * tq
            pv = _dot(p_ref[lo:, lo:hi], v_ref[0, hh, lo:hi, :])
            if c == 0:
                acc_ref[...] = pv
            else:
                acc_ref[lo:, :] += pv
        outs.append(acc_ref[:, :dv] / jnp.maximum(acc_ref[:, dv:dv + 1], 1e-30))
    o_ref[0] = jnp.concatenate(outs, axis=-1)


def _causal_attn(q, k, v, name, tq=256, hpb=2):
    B, H, S, dk = q.shape
    dv = v.shape[-1] // 2

    def heads(w):
        return pl.BlockSpec((1, hpb, S, w), lambda b, h: (b, h, 0, 0))

    return pl.pallas_call(
        functools.partial(_causal_attn_kernel, tq=tq),
        grid=(B, H // hpb),
        in_specs=[heads(dk), heads(dk), heads(2 * dv)],
        out_specs=pl.BlockSpec((1, S, hpb * dv), lambda b, h: (b, 0, h)),
        out_shape=jax.ShapeDtypeStruct((B, S, H * dv), F32),
        scratch_shapes=[pltpu.VMEM((S, S), F32), pltpu.VMEM((S, S), BF16), pltpu.VMEM((S, 2 * dv), F32)],
        compiler_params=_params(("parallel", "parallel")),
        name=name,
    )(q, k, v)


def _nsa_compress_kernel(t_ref, pos_ref, w1_ref, b1_ref, w2_ref, o_ref):
    st = NSA_CMP_STRIDE
    nrow = t_ref.shape[1] // st
    a = [jnp.zeros((nrow, NSA_CMP_HIDDEN), F32)] * 2
    b = [jnp.zeros((nrow, NSA_CMP_HIDDEN), F32)] * 2
    for l in range(st):
        rows = t_ref[0, pl.ds(l, nrow, stride=st), :]
        xa = (rows + pos_ref[l:l + 1, :]).astype(BF16)
        xb = (rows + pos_ref[st + l:st + l + 1, :]).astype(BF16)
        for j in range(2):
            a[j] = a[j] + _dot(xa, w1_ref[j, l * LANES:(l + 1) * LANES, :])
            b[j] = b[j] + _dot(xb, w1_ref[j, (st + l) * LANES:(st + l + 1) * LANES, :])
    for j in range(2):
        hid = jax.nn.gelu(a[j] + pltpu.roll(b[j], nrow - 1, 0) + b1_ref[j])
        o_ref[j, 0] = _dot(hid.astype(BF16), w2_ref[j]).astype(BF16)


def _nsa_compress(t_cmp, pos, w1, b1, w2, layer):
    B, S, _ = t_cmp.shape
    R = S // NSA_CMP_STRIDE
    d = HEAD_DIM

    def par(*shape):
        return pl.BlockSpec((None, 2) + shape, lambda b: (layer, 0, 0, 0))

    return pl.pallas_call(
        _nsa_compress_kernel,
        grid=(B,),
        in_specs=[
            pl.BlockSpec((1, S, LANES), lambda b: (b, 0, 0)),
            pl.BlockSpec((None, NSA_CMP_LEN, LANES), lambda b: (layer, 0, 0)),
            par(NSA_CMP_LEN * LANES, NSA_CMP_HIDDEN), par(1, NSA_CMP_HIDDEN), par(NSA_CMP_HIDDEN, d),
        ],
        out_specs=pl.BlockSpec((2, 1, R, d), lambda b: (0, b, 0, 0)),
        out_shape=jax.ShapeDtypeStruct((2, B, R, d), BF16),
        compiler_params=_params(("parallel",)),
        name="nsa_compress",
    )(t_cmp, pos, w1, b1, w2)


def _nsa_attn_kernel(q_ref, cmp_ref, ksl_ref, vsl_ref, kw_ref, vw_ref, gate_ref, ovl_ref, exp_ref, o_ref, slc_ref, *, tq):
    H = NSA_HEADS
    qi = pl.program_id(1)
    lo = qi * tq
    q4 = q_ref[0].reshape(H * tq, HEAD_DIM)
    pos_q = lo + lax.broadcasted_iota(jnp.int32, (tq, 1), 0)
    lane = lax.broadcasted_iota(jnp.int32, (1, LANES), 1)

    kc, vc = cmp_ref[0, 0], cmp_ref[1, 0]
    nc = kc.shape[0]
    cmp_end = lax.broadcasted_iota(jnp.int32, (1, nc), 1) * NSA_CMP_STRIDE + (NSA_CMP_LEN - 1)
    keep_c = jnp.where(pos_q >= cmp_end, 1.0, 0.0)
    p_c = _masked_softmax2(_dot_nt(q4, kc).reshape(H, tq, nc), keep_c)
    o_cmp = _dot(p_c.reshape(H * tq, nc).astype(BF16), vc).reshape(H, tq, HEAD_DIM)

    nw = NSA_WINDOW + tq
    start = pl.multiple_of(jnp.maximum(lo - NSA_WINDOW, 0), tq)
    kw = kw_ref[0, pl.ds(start, nw), :]
    vw = vw_ref[0, pl.ds(start, nw), :]
    kpos = start + lax.broadcasted_iota(jnp.int32, (1, nw), 1)
    keep_w = jnp.where((kpos <= pos_q) & (kpos > pos_q - NSA_WINDOW), 1.0, 0.0)
    s_w = _dot_nt(q4, kw).reshape(H, tq, nw) + ((1.0 - keep_w) * NEG_INF)[None]
    p_w = jnp.exp2((s_w - jnp.max(s_w, axis=-1, keepdims=True)).astype(BF16)) * keep_w.astype(BF16)[None]
    o_swa = _dot(p_w.reshape(H * tq, nw), vw).reshape(H, tq, 2 * HEAD_DIM)
    o_swa = o_swa[..., :HEAD_DIM] / jnp.maximum(o_swa[..., HEAD_DIM:HEAD_DIM + 1], 1e-30)

    p_sum = p_c[0] + p_c[1] + p_c[2] + p_c[3]
    hi, mid, lo3 = _split3(p_sum)
    ovl = ovl_ref[...]
    imp = _dot(hi, ovl) + _dot(mid, ovl) + _dot(lo3, ovl)
    cur = pos_q // NSA_SEL_LEN
    forced = (lane == 0) | (lane == cur) | (lane == cur - 1)
    score = jnp.where(forced, NSA_FORCE_SCORE, imp)
    score = jnp.where(lane <= cur, score, NEG_INF)
    n_blk = exp_ref.shape[1] // NSA_SEL_LEN
    score_t = score.T[:n_blk]
    blk = lax.broadcasted_iota(jnp.int32, (n_blk, 1), 0)
    cnt = jnp.zeros(score_t.shape, F32)
    for kk in range(n_blk):
        row = score_t[kk:kk + 1, :]
        beats = (row > score_t) | ((row == score_t) & (blk > kk))
        cnt = cnt + jnp.where(beats, 1.0, 0.0)
    sel_t = jnp.where(cnt < min(NSA_SEL_BLOCKS, n_blk), 1.0, 0.0)
    sel = jnp.concatenate([sel_t, jnp.zeros((LANES - n_blk, tq), F32)], axis=0).T.astype(BF16)

    keep_tri, _ = _tri_masks(tq)

    def slc_variant(i):
        lo_k, hi_k = i * tq, (i + 1) * tq
        keep = _dot(sel, exp_ref[:, :hi_k])
        m = jnp.full((H, tq, 1), NEG_INF, F32)
        acc = jnp.zeros((H, tq, 2 * HEAD_DIM), F32)
        for c in range(i + 1):
            ks = slice(c * tq, (c + 1) * tq)
            keep_c = keep[:, ks] * keep_tri if c == i else keep[:, ks]
            s = _dot_nt(q4, ksl_ref[0, ks, :]).reshape(H, tq, tq) + ((1.0 - keep_c) * NEG_INF)[None]
            m_new = jnp.maximum(m, jnp.max(s, axis=-1, keepdims=True))
            p = jnp.exp2((s - m_new).astype(BF16)) * keep_c.astype(BF16)[None]
            pv = _dot(p.reshape(H * tq, tq), vsl_ref[0, ks, :]).reshape(H, tq, 2 * HEAD_DIM)
            acc = jnp.exp2(m - m_new) * acc + pv
            m = m_new
        slc_ref[...] = acc[..., :HEAD_DIM] / jnp.maximum(acc[..., HEAD_DIM:HEAD_DIM + 1], 1e-30)

    for i in range(ksl_ref.shape[1] // tq):
        pl.when(qi == i)(functools.partial(slc_variant, i))
    o_slc = slc_ref[...]

    g = gate_ref[0]
    o_ref[0] = jnp.concatenate(
        [g[:, hh:hh + 1] * o_cmp[hh] + g[:, H + hh:H + hh + 1] * o_slc[hh] + g[:, 2 * H + hh:2 * H + hh + 1] * o_swa[hh]
         for hh in range(H)], axis=-1)


def _nsa_attn(q, cmp, ksl, vsl, kw, vw, gates, ovl, expand, tq=256):
    B, H, S, d = q.shape
    R = cmp.shape[2]

    def full(w):
        return pl.BlockSpec((1, S, w), lambda b, i: (b, 0, 0))

    return pl.pallas_call(
        functools.partial(_nsa_attn_kernel, tq=tq),
        grid=(B, S // tq),
        in_specs=[
            pl.BlockSpec((1, H, tq, d), lambda b, i: (b, 0, i, 0)),
            pl.BlockSpec((2, 1, R, d), lambda b, i: (0, b, 0, 0)),
            full(d), full(2 * d), full(d), full(2 * d),
            pl.BlockSpec((1, tq, LANES), lambda b, i: (b, i, 0)),
            pl.BlockSpec((R, LANES), lambda b, i: (0, 0)),
            pl.BlockSpec((LANES, S), lambda b, i: (0, 0)),
        ],
        out_specs=pl.BlockSpec((1, tq, H * d), lambda b, i: (b, i, 0)),
        out_shape=jax.ShapeDtypeStruct((B, S, H * d), F32),
        scratch_shapes=[pltpu.VMEM((H, tq, d), F32)],
        compiler_params=_params(("parallel", "arbitrary")),
        name="nsa_attn",
    )(q, cmp, ksl, vsl, kw, vw, gates, ovl, expand)


def _s5_kernel(u_ref, bre_ref, bim_ref, kr_ref, ki_ref, ar_ref, ai_ref, cre_ref, cim_ref, d_ref, y_ref,
               xr_ref, xi_ref, sr_ref, si_ref, us_ref, *, tc, nb, lane_blk):
    @pl.when(pl.program_id(0) == 0)
    def _():
        sr_ref[...] = jnp.zeros(sr_ref.shape, F32)
        si_ref[...] = jnp.zeros(si_ref.shape, F32)

    W = S5_WIDTH
    ncol = W // LANES
    for b in range(nb):
        for c in range(ncol):
            us_ref[c, pl.ds(b, tc, stride=nb), :] = u_ref[:, b * W + c * LANES:b * W + (c + 1) * LANES]
    u = jnp.concatenate([us_ref[c] for c in range(ncol)], axis=-1)
    ub = u.astype(BF16)
    kr, ki = kr_ref[...], ki_ref[...]
    bre, bim = bre_ref[...], bim_ref[...]
    xr_ref[...] = _dot(ub, (bre * kr - bim * ki).astype(BF16))
    xi_ref[...] = _dot(ub, (bre * ki + bim * kr).astype(BF16))

    for c in range(S5_STATES // lane_blk):
        sl = slice(c * lane_blk, (c + 1) * lane_blk)
        ar = jnp.broadcast_to(ar_ref[:, sl], (nb, lane_blk))
        ai = jnp.broadcast_to(ai_ref[:, sl], (nb, lane_blk))

        def step(t, carry):
            sr, si = carry
            row = pl.ds(pl.multiple_of(t * nb, nb), nb)
            nr = ar * sr - ai * si + xr_ref[row, sl]
            ni = ar * si + ai * sr + xi_ref[row, sl]
            xr_ref[row, sl] = nr
            xi_ref[row, sl] = ni
            return nr, ni

        sr, si = lax.fori_loop(0, tc, step, (sr_ref[:, sl], si_ref[:, sl]), unroll=4)
        sr_ref[:, sl] = sr
        si_ref[:, sl] = si

    y = (_dot(xr_ref[...].astype(BF16), cre_ref[...]) - _dot(xi_ref[...].astype(BF16), cim_ref[...])
         + d_ref[...] * u)
    for c in range(ncol):
        us_ref[c] = y[:, c * LANES:(c + 1) * LANES]
    for b in range(nb):
        for c in range(ncol):
            y_ref[:, b * W + c * LANES:b * W + (c + 1) * LANES] = us_ref[c, pl.ds(b, tc, stride=nb), :]


def _s5_scan(u_t, bre, bim, kr, ki, ar, ai, cre, cim, d, tc=128, lane_blk=512):
    S = u_t.shape[0]
    W = S5_WIDTH
    B = u_t.shape[1] // W
    const = lambda shape: pl.BlockSpec(shape, lambda i: (0, 0))
    return pl.pallas_call(
        functools.partial(_s5_kernel, tc=tc, nb=B, lane_blk=lane_blk),
        grid=(S // tc,),
        in_specs=[
            pl.BlockSpec((tc, B * W), lambda i: (i, 0)),
            const((W, S5_STATES)), const((W, S5_STATES)),
            const((1, S5_STATES)), const((1, S5_STATES)), const((1, S5_STATES)), const((1, S5_STATES)),
            const((S5_STATES, W)), const((S5_STATES, W)), const((1, W)),
        ],
        out_specs=pl.BlockSpec((tc, B * W), lambda i: (i, 0)),
        out_shape=jax.ShapeDtypeStruct((S, B * W), F32),
        scratch_shapes=[
            pltpu.VMEM((tc * B, S5_STATES), F32), pltpu.VMEM((tc * B, S5_STATES), F32),
            pltpu.VMEM((B, S5_STATES), F32), pltpu.VMEM((B, S5_STATES), F32),
            pltpu.VMEM((W // LANES, tc * B, LANES), F32),
        ],
        compiler_params=_params(("arbitrary",)),
        name="s5_scan",
    )(u_t, bre, bim, kr, ki, ar, ai, cre, cim, d)


def _mix_out_kernel(h_ref, om_ref, on_ref, of_ref, ys_ref, gw_ref, gb_ref, gg_ref, wo_ref, pg_ref, o_ref):
    gg = gg_ref[0]
    y = jax.nn.gelu(ys_ref[...])
    y = y * jax.nn.sigmoid(_dot(y.astype(BF16), gw_ref[0]) + gb_ref[...])
    groups = (om_ref[0], on_ref[0], of_ref[0], y)
    cat = jnp.concatenate([_rms(p, gg[i:i + 1, :]).astype(BF16) for i, p in enumerate(groups)], axis=-1)
    o_ref[0] = h_ref[0] + _rms(_dot(cat, wo_ref[0]), pg_ref[...])


def _mix_out(h, o_mla, o_nsa, o_fox, y_s5, glu_w, glu_b, group_g, w_out, post_g, layer, tm=1024):
    B, S, D = h.shape

    def group():
        return pl.BlockSpec((1, tm, GROUP_WIDTH), lambda b, t: (b, t, 0))

    return pl.pallas_call(
        _mix_out_kernel,
        grid=(B, S // tm),
        in_specs=[
            pl.BlockSpec((1, tm, D), lambda b, t: (b, t, 0)),
            group(), group(), group(),
            pl.BlockSpec((tm, S5_WIDTH), lambda b, t: (t, b)),
            pl.BlockSpec((1, S5_WIDTH, S5_WIDTH), lambda b, t: (layer, 0, 0)),
            pl.BlockSpec((None, 1, S5_WIDTH), lambda b, t: (layer, 0, 0)),
            pl.BlockSpec((1, 4, GROUP_WIDTH), lambda b, t: (layer, 0, 0)),
            pl.BlockSpec((1, D, D), lambda b, t: (layer, 0, 0)),
            pl.BlockSpec((None, 1, D), lambda b, t: (layer, 0, 0)),
        ],
        out_specs=pl.BlockSpec((1, tm, D), lambda b, t: (b, t, 0)),
        out_shape=jax.ShapeDtypeStruct((B, S, D), F32),
        compiler_params=_params(("parallel", "parallel")),
        name="mix_out",
    )(h, o_mla, o_nsa, o_fox, y_s5, glu_w, glu_b, group_g, w_out, post_g)


def _xa_kv_kernel(mem_ref, g_ref, wkv_ref, kv_ref):
    m = _rms(mem_ref[0], g_ref[...]).astype(BF16)
    kv_ref[0] = _dot(m, wkv_ref[0]).astype(BF16)


def _xa_kv(mem, g, wkv, layer):
    B, M, D = mem.shape
    return pl.pallas_call(
        _xa_kv_kernel,
        grid=(B,),
        in_specs=[
            pl.BlockSpec((1, M, D), lambda b: (b, 0, 0)),
            pl.BlockSpec((None, 1, D), lambda b: (layer, 0, 0)),
            pl.BlockSpec((1, D, 2 * D), lambda b: (layer, 0, 0)),
        ],
        out_specs=pl.BlockSpec((1, M, 2 * D), lambda b: (b, 0, 0)),
        out_shape=jax.ShapeDtypeStruct((B, M, 2 * D), BF16),
        compiler_params=_params(("parallel",)),
        name="xa_kv",
    )(mem, g, wkv)


def _xa_kernel(h_ref, kv_ref, pre_g_ref, wq_ref, wo_ref, post_g_ref, o_ref):
    h = h_ref[0]
    u = _rms(h, pre_g_ref[...]).astype(BF16)
    q = _dot(u, wq_ref[0]) * (XA_HEAD_DIM ** -0.5)
    acc = jnp.zeros(h.shape, F32)
    for hh in range(XA_HEADS):
        sl = slice(hh * XA_HEAD_DIM, (hh + 1) * XA_HEAD_DIM)
        k = kv_ref[0, :, sl]
        v = kv_ref[0, :, D_MODEL + hh * XA_HEAD_DIM:D_MODEL + (hh + 1) * XA_HEAD_DIM]
        s = _dot_nt(q[:, sl].astype(BF16), k)
        m = jnp.max(s, axis=-1, keepdims=True)
        e = jnp.exp(s - m)
        p = e / jnp.sum(e, axis=-1, keepdims=True)
        o = _dot(p.astype(BF16), v)
        acc = acc + _dot(o.astype(BF16), wo_ref[0, sl, :])
    o_ref[0] = h + _rms(acc, post_g_ref[...])


def _xa(h, kv, pre_g, wq, wo, post_g, layer, tm=1024):
    B, S, D = h.shape
    M = kv.shape[1]
    return pl.pallas_call(
        _xa_kernel,
        grid=(B, S // tm),
        in_specs=[
            pl.BlockSpec((1, tm, D), lambda b, t: (b, t, 0)),
            pl.BlockSpec((1, M, 2 * D), lambda b, t: (b, 0, 0)),
            pl.BlockSpec((None, 1, D), lambda b, t: (layer, 0, 0)),
            pl.BlockSpec((1, D, D), lambda b, t: (layer, 0, 0)),
            pl.BlockSpec((1, D, D), lambda b, t: (layer, 0, 0)),
            pl.BlockSpec((None, 1, D), lambda b, t: (layer, 0, 0)),
        ],
        out_specs=pl.BlockSpec((1, tm, D), lambda b, t: (b, t, 0)),
        out_shape=jax.ShapeDtypeStruct((B, S, D), F32),
        compiler_params=_params(("parallel", "parallel")),
        name="cross_attn",
    )(h, kv, pre_g, wq, wo, post_g)


def _rope_tables(n, dim):
    inv = 1.0 / (ROPE_THETA ** (jnp.arange(0, dim, 2, dtype=F32) / dim))
    ang = jnp.arange(n, dtype=F32)[:, None] * inv[None, :]
    return jnp.cos(ang), jnp.sin(ang)


def _pack_w_in(w):
    L, D, _ = w.shape
    z = lambda n: jnp.zeros((L, D, n), w.dtype)
    o = 0
    cuts = {}
    for name, n in (("cq", MLA_Q_LORA), ("ckv", MLA_KV_LORA), ("kpe", MLA_ROPE_DIM), ("nq", 256), ("kc", 64),
                    ("vc", 64), ("ksl", 64), ("vsl", 64), ("kw", 64), ("vw", 64), ("gn", N_GATES),
                    ("fq", 256), ("fk", 256), ("fv", 256), ("ff", FOX_HEADS), ("s5", S5_WIDTH)):
        cuts[name] = w[:, :, o:o + n]
        o += n
    c = cuts
    return jnp.concatenate([
        c["cq"], z(64), c["ckv"], z(MLA_NOPE_DIM), c["kpe"], z(32), c["nq"], c["kc"], c["vc"], c["ksl"], c["vsl"],
        c["kw"], c["vw"], c["gn"], c["ff"], z(LANES - N_GATES - FOX_HEADS), c["fq"], c["fk"], c["fv"], c["s5"],
    ], axis=-1)


def kernel(x, mem, ffn1_pre_g, ffn1_w_gate, ffn1_w_up, ffn1_w_down, ffn1_post_g, mix_pre_g, mix_w_in, mla_q_norm_g, mla_w_uq, mla_kv_norm_g, mla_w_ukv, nsa_cmp_pos, nsa_phi_w1, nsa_phi_b1, nsa_phi_w2, nsa_gate_b, fox_f_b, s5_lambda_re, s5_lambda_im, s5_log_dt, s5_b_re, s5_b_im, s5_c_re, s5_c_im, s5_d, s5_glu_w, s5_glu_b, mix_group_g, mix_w_out, mix_post_g, xa_pre_g, xa_mem_g, xa_w_q, xa_w_kv, xa_w_o, xa_post_g, ffn2_pre_g, ffn2_w_gate, ffn2_w_up, ffn2_w_down, ffn2_post_g):
    B, S, D = x.shape
    L = ffn1_pre_g.shape[0]
    T = B * S
    G, P, C = S5_GROUPS, S5_STATE, S5_GROUP
    bf = lambda a: a.astype(BF16)

    w_in = bf(_pack_w_in(mix_w_in))
    qg = jnp.pad(mla_q_norm_g, ((0, 0), (0, 256 - MLA_Q_LORA)))
    wuq = mla_w_uq.reshape(L, MLA_Q_LORA, MLA_HEADS, MLA_NOPE_DIM + MLA_ROPE_DIM)
    wuq = jnp.pad(wuq, ((0, 0), (0, 256 - MLA_Q_LORA), (0, 0), (0, MLA_PAD_DIM - MLA_NOPE_DIM - MLA_ROPE_DIM)))
    wuq = bf(wuq.reshape(L, 256, MLA_HEADS * MLA_PAD_DIM))
    wukv = mla_w_ukv.reshape(L, MLA_KV_LORA, MLA_HEADS, MLA_NOPE_DIM + MLA_V_DIM)
    wk = jnp.pad(wukv[..., :MLA_NOPE_DIM], ((0, 0), (0, 0), (0, 0), (0, MLA_PAD_DIM - MLA_NOPE_DIM)))
    wukv = bf(jnp.concatenate([wk.reshape(L, MLA_KV_LORA, -1), wukv[..., MLA_NOPE_DIM:].reshape(L, MLA_KV_LORA, -1)], axis=-1))
    misc_bias = jnp.concatenate([nsa_gate_b, fox_f_b, jnp.zeros((L, LANES - N_GATES - FOX_HEADS), F32)], axis=-1)

    c32, s32 = _rope_tables(S, HEAD_DIM)
    cos64 = jnp.tile(jnp.concatenate([c32, c32], axis=-1), (1, 2))
    sin64 = jnp.tile(jnp.concatenate([-s32, s32], axis=-1), (1, 2))
    c16, s16 = _rope_tables(S, MLA_ROPE_DIM)
    one, zero = jnp.ones((S, 1), F32), jnp.zeros((S, 1), F32)
    cosm = jnp.concatenate([jnp.tile(one, (1, 64)), c16, c16, jnp.tile(one, (1, 32))], axis=-1)
    sinm = jnp.concatenate([jnp.tile(zero, (1, 64)), -s16, s16, jnp.tile(zero, (1, 32))], axis=-1)

    cmp_pos = jnp.concatenate([nsa_cmp_pos[:, 0], nsa_cmp_pos[:, 1]], axis=-1)
    w1 = nsa_phi_w1.reshape(L, 2, NSA_CMP_LEN, HEAD_DIM, NSA_CMP_HIDDEN)
    w1 = jnp.stack([jnp.pad(w1[:, 0], ((0, 0), (0, 0), (0, HEAD_DIM), (0, 0))),
                    jnp.pad(w1[:, 1], ((0, 0), (0, 0), (HEAD_DIM, 0), (0, 0)))], axis=1)
    phi_w1 = bf(w1.reshape(L, 2, NSA_CMP_LEN * LANES, NSA_CMP_HIDDEN))
    phi_w2 = bf(nsa_phi_w2)
    phi_b1 = nsa_phi_b1.reshape(L, 2, 1, NSA_CMP_HIDDEN)

    n_cmp_pad = S // NSA_CMP_STRIDE
    n_blk = S // NSA_SEL_LEN
    cs = np.arange(n_cmp_pad) * NSA_CMP_STRIDE
    ss = np.arange(LANES) * NSA_SEL_LEN
    ovl_np = ((cs[:, None] < ss[None, :] + NSA_SEL_LEN) & (cs[:, None] + NSA_CMP_LEN > ss[None, :])
              & (np.arange(LANES)[None, :] < n_blk) & (np.arange(n_cmp_pad)[:, None] < n_cmp_pad - 1))
    ovl = jnp.asarray(ovl_np.astype(np.float32), dtype=BF16)
    tq_nsa = 256
    key_blk = np.arange(S) // NSA_SEL_LEN
    expand = jnp.asarray((key_blk[None, :] == np.arange(LANES)[:, None]).astype(np.float32), dtype=BF16)

    dt = jnp.exp(s5_log_dt)[:, :, None]
    lr, li = s5_lambda_re, s5_lambda_im
    mag = jnp.exp(lr * dt)
    a_r, a_i = mag * jnp.cos(li * dt), mag * jnp.sin(li * dt)
    den = lr * lr + li * li
    k_r = ((a_r - 1.0) * lr + a_i * li) / den
    k_i = (a_i * lr - (a_r - 1.0) * li) / den
    flat = lambda a: a.reshape(L, 1, G * P)
    a_r, a_i, k_r, k_i = flat(a_r), flat(a_i), flat(k_r), flat(k_i)
    eye = jnp.eye(G, dtype=F32)
    blk_b = lambda b: jnp.einsum('lgpc,gh->lgchp', b, eye).reshape(L, G * C, G * P)
    blk_c = lambda c: jnp.einsum('lgcp,gh->lgphc', c, eye).reshape(L, G * P, G * C)
    b_re, b_im = blk_b(s5_b_re), blk_b(s5_b_im)
    c_re, c_im = bf(blk_c(s5_c_re)), bf(blk_c(s5_c_im))
    d_skip = s5_d.reshape(L, 1, S5_WIDTH)

    ffn_w = [(bf(ffn1_w_gate), bf(ffn1_w_up), bf(ffn1_w_down)), (bf(ffn2_w_gate), bf(ffn2_w_up), bf(ffn2_w_down))]
    glu_w, w_out = bf(s5_glu_w), bf(mix_w_out)
    wq_x, wkv_x, wo_x = bf(xa_w_q), bf(xa_w_kv), bf(xa_w_o)

    v3 = lambda a: a.reshape(L, 1, a.shape[-1])
    ffn1_pre_g, ffn1_post_g, ffn2_pre_g, ffn2_post_g = v3(ffn1_pre_g), v3(ffn1_post_g), v3(ffn2_pre_g), v3(ffn2_post_g)
    mix_pre_g, mix_post_g, qg, kvg, misc_bias = v3(mix_pre_g), v3(mix_post_g), v3(qg), v3(mla_kv_norm_g), v3(misc_bias)
    s5_glu_b, xa_mem_g, xa_pre_g, xa_post_g = v3(s5_glu_b), v3(xa_mem_g), v3(xa_pre_g), v3(xa_post_g)

    h = x
    for l in range(L):
        h = _ffn(h.reshape(T, D), ffn1_pre_g, *ffn_w[0], ffn1_post_g, l).reshape(B, S, D)

        (mq, mk, mv, nq, ncmp, nksl, nvsl, nkw, nvw, misc, fq, fk, fv, s5u) = _proj(
            h, mix_pre_g, w_in, qg, wuq, kvg, wukv, misc_bias, cos64, sin64, cosm, sinm, l)

        o_mla = _causal_attn(mq, mk, mv, "mla_attn")

        o_fox = _causal_attn(fq, fk, fv, "fox_attn")

        cmp = _nsa_compress(ncmp, cmp_pos, phi_w1, phi_b1, phi_w2, l)
        o_nsa = _nsa_attn(nq, cmp, nksl, nvsl, nkw, nvw, misc, ovl, expand, tq=tq_nsa)

        y_s5 = _s5_scan(s5u, b_re[l], b_im[l], k_r[l], k_i[l], a_r[l], a_i[l],
                        c_re[l], c_im[l], d_skip[l])

        h = _mix_out(h, o_mla, o_nsa, o_fox, y_s5, glu_w, s5_glu_b, mix_group_g, w_out, mix_post_g, l)

        kv = _xa_kv(mem, xa_mem_g, wkv_x, l)
        h = _xa(h, kv, xa_pre_g, wq_x, wo_x, xa_post_g, l)

        h = _ffn(h.reshape(T, D), ffn2_pre_g, *ffn_w[1], ffn2_post_g, l).reshape(B, S, D)
    return h
```

```python
import functools
import math

import numpy as np
import jax
import jax.numpy as jnp
from jax import lax
from jax.experimental import pallas as pl
from jax.experimental.pallas import tpu as pltpu

F32 = jnp.float32
BF16 = jnp.bfloat16

D_MODEL = 1024
HEAD_DIM = 64
ROPE_THETA = 10000.0
NORM_EPS = 1e-6
NEG_INF = -1e30
LOG2E = math.log2(math.e)
MACARON_WEIGHT = 0.5
FFN_DIM = 2816
XA_HEADS = 4
XA_HEAD_DIM = D_MODEL // XA_HEADS

MLA_HEADS = 4
MLA_Q_LORA = 192
MLA_KV_LORA = 128
MLA_NOPE_DIM = 64
MLA_ROPE_DIM = 32
MLA_V_DIM = 64
MLA_PAD_DIM = 128

NSA_HEADS = 4
NSA_CMP_LEN = 32
NSA_CMP_STRIDE = 16
NSA_CMP_HIDDEN = 256
NSA_SEL_LEN = 64
NSA_SEL_BLOCKS = 8
NSA_WINDOW = 256
NSA_FORCE_SCORE = 1e4

FOX_HEADS = 4

S5_WIDTH = 256
S5_GROUP = 16
S5_GROUPS = S5_WIDTH // S5_GROUP
S5_STATE = 64
S5_STATES = S5_GROUPS * S5_STATE

GROUP_WIDTH = 256
LANES = 128
CUMSUM_BLOCK = 256
VMEM_LIMIT = 56 * 1024 * 1024

SEG_CQ, SEG_CKV, SEG_KPE, SEG_NQ = 0, 256, 384, 512
SEG_NCMP, SEG_NSLC, SEG_NSWA, SEG_MISC = 768, 896, 1024, 1152
SEG_FQ, SEG_FK, SEG_FV, SEG_S5, IN_PACKED = 1280, 1536, 1792, 2048, 2304
N_GATES = 3 * NSA_HEADS


def _params(sem):
    return pltpu.CompilerParams(dimension_semantics=sem, vmem_limit_bytes=VMEM_LIMIT)


def _rms(x, g, n=None):
    n = x.shape[-1] if n is None else n
    ms = jnp.sum(x * x, axis=-1, keepdims=True) * (1.0 / n)
    return x * lax.rsqrt(ms + NORM_EPS) * g


def _dot(a, b):
    return jnp.dot(a, b, preferred_element_type=F32)


def _dot_nt(a, b):
    return lax.dot_general(a, b, (((1,), (1,)), ((), ())), preferred_element_type=F32)


def _split3(x):
    hi = x.astype(BF16)
    r = x - hi.astype(F32)
    mid = r.astype(BF16)
    lo = (r - mid.astype(F32)).astype(BF16)
    return hi, mid, lo


def _masked_softmax2(s, keep):
    s = s + ((1.0 - keep) * NEG_INF)[None]
    m = jnp.max(s, axis=-1, keepdims=True)
    e = jnp.exp2(s - m) * keep[None]
    return e / jnp.maximum(jnp.sum(e, axis=-1, keepdims=True), 1e-30)


def _ffn_kernel(h_ref, pre_g_ref, wg_ref, wu_ref, wd_ref, post_g_ref, o_ref, *, chunk):
    h = h_ref[...]
    u = _rms(h, pre_g_ref[...]).astype(BF16)
    acc = jnp.zeros(h.shape, F32)
    for c in range(wg_ref.shape[1] // chunk):
        sl = slice(c * chunk, (c + 1) * chunk)
        g = _dot(u, wg_ref[:, sl])
        p = _dot(u, wu_ref[:, sl])
        a = (g * jax.nn.sigmoid(g)) * p
        acc = acc + _dot(a.astype(BF16), wd_ref[sl, :])
    o_ref[...] = h + MACARON_WEIGHT * _rms(acc, post_g_ref[...])


def _ffn(h, pre_g, wg, wu, wd, post_g, layer, tm=1024, chunk=256):
    T, D = h.shape
    F = wg.shape[-1]
    wspec = functools.partial(pl.BlockSpec, pipeline_mode=pl.Buffered(1))
    return pl.pallas_call(
        functools.partial(_ffn_kernel, chunk=chunk),
        grid=(T // tm,),
        in_specs=[
            pl.BlockSpec((tm, D), lambda i: (i, 0)),
            pl.BlockSpec((None, 1, D), lambda i: (layer, 0, 0)),
            wspec((None, D, F), lambda i: (layer, 0, 0)),
            wspec((None, D, F), lambda i: (layer, 0, 0)),
            wspec((None, F, D), lambda i: (layer, 0, 0)),
            pl.BlockSpec((None, 1, D), lambda i: (layer, 0, 0)),
        ],
        out_specs=pl.BlockSpec((tm, D), lambda i: (i, 0)),
        out_shape=jax.ShapeDtypeStruct((T, D), F32),
        compiler_params=_params(("parallel",)),
        name="ffn",
    )(h, pre_g, wg, wu, wd, post_g)


def _proj_kernel(h_ref, g_ref, win_ref, qg_ref, wuq_ref, kvg_ref, wukv_ref, bias_ref,
                 cos64_ref, sin64_ref, cosm_ref, sinm_ref,
                 mq_ref, mk_ref, mv_ref, nq_ref, ncmp_ref, nksl_ref, nvsl_ref, nkw_ref, nvw_ref,
                 misc_ref, fq_ref, fk_ref, fv_ref, s5_ref, carry_ref):
    h = h_ref[0]
    tm = h.shape[0]
    u = _rms(h, g_ref[...]).astype(BF16)
    z = _dot(u, win_ref[...])
    lane = lax.broadcasted_iota(jnp.int32, (tm, LANES), 1)

    def rope(x, c, s, half):
        fwd = pltpu.roll(x, LANES - half, 1)
        bwd = pltpu.roll(x, half, 1)
        return x * c + jnp.where(lane % (2 * half) < half, fwd, bwd) * s

    def padded(x, j, fill):
        blk = x[:, (j // 2) * LANES:(j // 2 + 1) * LANES]
        if j % 2:
            blk = pltpu.roll(blk, HEAD_DIM, 1)
        return jnp.where(lane < HEAD_DIM, blk, fill).astype(BF16)

    def with_ones(x, j):
        return padded(x, j, 1.0)

    cos64, sin64 = cos64_ref[...], sin64_ref[...]
    cosm, sinm = cosm_ref[...], sinm_ref[...]
    half_m = MLA_ROPE_DIM // 2
    half_n = HEAD_DIM // 2

    cq = _rms(z[:, SEG_CQ:SEG_CKV], qg_ref[...], MLA_Q_LORA).astype(BF16)
    q = _dot(cq, wuq_ref[...])
    ckv = _rms(z[:, SEG_CKV:SEG_KPE], kvg_ref[...]).astype(BF16)
    kv = _dot(ckv, wukv_ref[...])
    kpe = rope(z[:, SEG_KPE:SEG_NQ], cosm, sinm, half_m)
    mla_scale = (MLA_NOPE_DIM + MLA_ROPE_DIM) ** -0.5 * LOG2E
    for hh in range(MLA_HEADS):
        sl = slice(hh * MLA_PAD_DIM, (hh + 1) * MLA_PAD_DIM)
        mq_ref[0, hh] = (rope(q[:, sl], cosm, sinm, half_m) * mla_scale).astype(BF16)
        mk_ref[0, hh] = (kv[:, sl] + kpe).astype(BF16)
        mv_ref[0, hh] = with_ones(kv[:, MLA_HEADS * MLA_PAD_DIM:], hh)

    scale = HEAD_DIM ** -0.5 * LOG2E
    for c in range(NSA_HEADS // 2):
        r = rope(z[:, SEG_NQ + c * LANES:SEG_NQ + (c + 1) * LANES], cos64, sin64, half_n) * scale
        nq_ref[0, 2 * c] = r[:, :HEAD_DIM].astype(BF16)
        nq_ref[0, 2 * c + 1] = r[:, HEAD_DIM:].astype(BF16)
    seg = z[:, SEG_NCMP:SEG_NSLC]
    ncmp_ref[0] = jnp.where(lane < HEAD_DIM, rope(seg, cos64, sin64, half_n), seg)
    seg = z[:, SEG_NSLC:SEG_NSWA]
    nksl_ref[0] = rope(seg, cos64, sin64, half_n)[:, :HEAD_DIM].astype(BF16)
    nvsl_ref[0] = with_ones(seg, 1)
    seg = z[:, SEG_NSWA:SEG_MISC]
    nkw_ref[0] = rope(seg, cos64, sin64, half_n)[:, :HEAD_DIM].astype(BF16)
    nvw_ref[0] = with_ones(seg, 1)

    x = z[:, SEG_MISC:SEG_FQ] + bias_ref[...]
    log_sig = jnp.minimum(x, 0.0) - jnp.log(1.0 + jnp.exp(-jnp.abs(x)))
    misc = jnp.where(lane < N_GATES, jax.nn.sigmoid(x), log_sig * LOG2E)
    misc_ref[0] = misc

    @pl.when(pl.program_id(1) == 0)
    def _():
        carry_ref[...] = jnp.zeros(carry_ref.shape, F32)

    d = HEAD_DIM
    blk = CUMSUM_BLOCK
    r = lax.broadcasted_iota(jnp.int32, (blk, blk), 0)
    c = lax.broadcasted_iota(jnp.int32, (blk, blk), 1)
    tri = jnp.where(r >= c, 1.0, 0.0).astype(BF16)
    lane_b = lax.broadcasted_iota(jnp.int32, (blk, LANES), 1)
    nh = FOX_HEADS
    span = 3 * nh
    first, second = (lane_b >= d) & (lane_b < d + span), (lane_b >= d + span) & (lane_b < d + 2 * span)
    carry = carry_ref[...]
    for i in range(tm // blk):
        rows = slice(i * blk, (i + 1) * blk)
        hi, mid, lo = _split3(misc[rows])
        cs = _dot(tri, hi) + _dot(tri, mid) + _dot(tri, lo) + carry
        carry = cs[blk - 1:blk, :]
        val = jnp.zeros((blk, LANES), F32)
        for j, p in enumerate(_split3(cs)):
            grp = (lane_b >= d + nh * j) & (lane_b < d + nh * (j + 1))
            val = jnp.where(grp, pltpu.roll(p.astype(F32), d + nh * j - N_GATES, 1), val)
        qa = jnp.where(second, 1.0, val)
        ka = jnp.where(first, 1.0, jnp.where(second, -pltpu.roll(val, span, 1), 0.0))
        for hh in range(FOX_HEADS):
            own = (lane_b - d) % nh == hh
            piece = slice((hh // 2) * LANES, (hh // 2 + 1) * LANES)
            q_h, k_h = z[rows, SEG_FQ:SEG_FK][:, piece] * scale, z[rows, SEG_FK:SEG_FV][:, piece]
            if hh % 2:
                q_h, k_h = pltpu.roll(q_h, d, 1), pltpu.roll(k_h, d, 1)
            fq_ref[0, hh, rows, :] = jnp.where(lane_b < d, q_h, qa).astype(BF16)
            fk_ref[0, hh, rows, :] = jnp.where(lane_b < d, k_h, jnp.where(own, ka, 0.0)).astype(BF16)
    carry_ref[...] = carry
    for hh in range(FOX_HEADS):
        fv_ref[0, hh] = with_ones(z[:, SEG_FV:SEG_S5], hh)

    s5_ref[...] = z[:, SEG_S5:IN_PACKED]


def _proj(h, g, win, qg, wuq, kvg, wukv, bias, cos64, sin64, cosm, sinm, layer, tm=1024):
    B, S, D = h.shape
    nt = S // tm
    wspec = functools.partial(pl.BlockSpec, pipeline_mode=pl.Buffered(1))

    def w2(shape):
        return wspec((None,) + shape, lambda b, t: (layer, 0, 0))

    def vec(n):
        return pl.BlockSpec((None, 1, n), lambda b, t: (layer, 0, 0))

    def tab():
        return pl.BlockSpec((tm, LANES), lambda b, t: (t, 0))

    def heads(d):
        return pl.BlockSpec((1, 4, tm, d), lambda b, t: (b, 0, t, 0))

    def tok(d):
        return pl.BlockSpec((1, tm, d), lambda b, t: (b, t, 0))

    def hshape(d, dt=BF16):
        return jax.ShapeDtypeStruct((B, 4, S, d), dt)

    def tshape(d, dt):
        return jax.ShapeDtypeStruct((B, S, d), dt)

    return pl.pallas_call(
        _proj_kernel,
        grid=(B, nt),
        in_specs=[
            tok(D), vec(D), w2((D, IN_PACKED)), vec(256), w2((256, 4 * MLA_PAD_DIM)),
            vec(MLA_KV_LORA), w2((MLA_KV_LORA, 4 * MLA_PAD_DIM + 4 * MLA_V_DIM)), vec(LANES),
            tab(), tab(), tab(), tab(),
        ],
        out_specs=[
            heads(MLA_PAD_DIM), heads(MLA_PAD_DIM), heads(LANES),
            heads(HEAD_DIM), tok(LANES), tok(HEAD_DIM), tok(LANES), tok(HEAD_DIM), tok(LANES),
            tok(LANES), heads(LANES), heads(LANES), heads(LANES),
            pl.BlockSpec((tm, S5_WIDTH), lambda b, t: (t, b)),
        ],
        out_shape=[
            hshape(MLA_PAD_DIM), hshape(MLA_PAD_DIM), hshape(LANES),
            hshape(HEAD_DIM), tshape(LANES, F32),
            tshape(HEAD_DIM, BF16), tshape(LANES, BF16), tshape(HEAD_DIM, BF16), tshape(LANES, BF16),
            tshape(LANES, F32), hshape(LANES), hshape(LANES), hshape(LANES),
            jax.ShapeDtypeStruct((S, B * S5_WIDTH), F32),
        ],
        scratch_shapes=[pltpu.VMEM((1, LANES), F32)],
        compiler_params=_params(("parallel", "arbitrary")),
        name="mix_proj",
    )(h, g, win, qg, wuq, kvg, wukv, bias, cos64, sin64, cosm, sinm)


def _tri_masks(tq):
    r = lax.broadcasted_iota(jnp.int32, (tq, tq), 0)
    c = lax.broadcasted_iota(jnp.int32, (tq, tq), 1)
    keep = jnp.where(r >= c, 1.0, 0.0)
    return keep, (1.0 - keep) * NEG_INF


def _causal_attn_kernel(q_ref, k_ref, v_ref, o_ref, s_ref, p_ref, acc_ref, *, tq):
    S = q_ref.shape[2]
    dv = v_ref.shape[-1] // 2
    n = S // tq
    keep, bias_tri = _tri_masks(tq)
    keep_b = keep.astype(BF16)
    outs = []
    for hh in range(q_ref.shape[1]):
        for c in range(n):
            lo, hi = c * tq, (c + 1) * tq
            s = _dot_nt(q_ref[0, hh, lo:, :], k_ref[0, hh, lo:hi, :])
            s_ref[lo:hi, lo:hi] = s[:tq] + bias_tri
            if c < n - 1:
                s_ref[hi:, lo:hi] = s[tq:]
        for i in range(n):
            lo, hi = i * tq, (i + 1) * tq
            s_d = s_ref[lo:hi, lo:hi]
            m = jnp.max(s_d, axis=-1, keepdims=True)
            if i > 0:
                s_o = s_ref[lo:hi, :lo]
                m = jnp.maximum(m, jnp.max(s_o, axis=-1, keepdims=True))
                p_ref[lo:hi, :lo] = jnp.exp2((s_o - m).astype(BF16))
            p_ref[lo:hi, lo:hi] = jnp.exp2((s_d - m).astype(BF16)) * keep_b
        for c in range(n):
            lo, hi = c * tq, (c + 1) * tq
            pv = _dot(p_ref[lo:, lo:hi], v_ref[0, hh, lo:hi, :])
            if c == 0:
                acc_ref[...] = pv
            else:
                acc_ref[lo:, :] += pv
        outs.append(acc_ref[:, :dv] / jnp.maximum(acc_ref[:, dv:dv + 1], 1e-30))
    o_ref[0] = jnp.concatenate(outs, axis=-1)


def _causal_attn(q, k, v, name, tq=256, hpb=2):
    B, H, S, dk = q.shape
    dv = v.shape[-1] // 2

    def heads(w):
        return pl.BlockSpec((1, hpb, S, w), lambda b, h: (b, h, 0, 0))

    return pl.pallas_call(
        functools.partial(_causal_attn_kernel, tq=tq),
        grid=(B, H // hpb),
        in_specs=[heads(dk), heads(dk), heads(2 * dv)],
        out_specs=pl.BlockSpec((1, S, hpb * dv), lambda b, h: (b, 0, h)),
        out_shape=jax.ShapeDtypeStruct((B, S, H * dv), F32),
        scratch_shapes=[pltpu.VMEM((S, S), F32), pltpu.VMEM((S, S), BF16), pltpu.VMEM((S, 2 * dv), F32)],
        compiler_params=_params(("parallel", "parallel")),
        name=name,
    )(q, k, v)


def _nsa_compress_kernel(t_ref, pos_ref, w1_ref, b1_ref, w2_ref, o_ref):
    st = NSA_CMP_STRIDE
    nrow = t_ref.shape[1] // st
    a = [jnp.zeros((nrow, NSA_CMP_HIDDEN), F32)] * 2
    b = [jnp.zeros((nrow, NSA_CMP_HIDDEN), F32)] * 2
    for l in range(st):
        rows = t_ref[0, pl.ds(l, nrow, stride=st), :]
        xa = (rows + pos_ref[l:l + 1, :]).astype(BF16)
        xb = (rows + pos_ref[st + l:st + l + 1, :]).astype(BF16)
        for j in range(2):
            a[j] = a[j] + _dot(xa, w1_ref[j, l * LANES:(l + 1) * LANES, :])
            b[j] = b[j] + _dot(xb, w1_ref[j, (st + l) * LANES:(st + l + 1) * LANES, :])
    for j in range(2):
        hid = jax.nn.gelu(a[j] + pltpu.roll(b[j], nrow - 1, 0) + b1_ref[j])
        o_ref[j, 0] = _dot(hid.astype(BF16), w2_ref[j]).astype(BF16)


def _nsa_compress(t_cmp, pos, w1, b1, w2, layer):
    B, S, _ = t_cmp.shape
    R = S // NSA_CMP_STRIDE
    d = HEAD_DIM

    def par(*shape):
        return pl.BlockSpec((None, 2) + shape, lambda b: (layer, 0, 0, 0))

    return pl.pallas_call(
        _nsa_compress_kernel,
        grid=(B,),
        in_specs=[
            pl.BlockSpec((1, S, LANES), lambda b: (b, 0, 0)),
            pl.BlockSpec((None, NSA_CMP_LEN, LANES), lambda b: (layer, 0, 0)),
            par(NSA_CMP_LEN * LANES, NSA_CMP_HIDDEN), par(1, NSA_CMP_HIDDEN), par(NSA_CMP_HIDDEN, d),
        ],
        out_specs=pl.BlockSpec((2, 1, R, d), lambda b: (0, b, 0, 0)),
        out_shape=jax.ShapeDtypeStruct((2, B, R, d), BF16),
        compiler_params=_params(("parallel",)),
        name="nsa_compress",
    )(t_cmp, pos, w1, b1, w2)


def _nsa_attn_kernel(q_ref, cmp_ref, ksl_ref, vsl_ref, kw_ref, vw_ref, gate_ref, ovl_ref, exp_ref, o_ref, slc_ref, *, tq):
    H = NSA_HEADS
    qi = pl.program_id(1)
    lo = qi * tq
    q4 = q_ref[0].reshape(H * tq, HEAD_DIM)
    pos_q = lo + lax.broadcasted_iota(jnp.int32, (tq, 1), 0)
    lane = lax.broadcasted_iota(jnp.int32, (1, LANES), 1)

    kc, vc = cmp_ref[0, 0], cmp_ref[1, 0]
    nc = kc.shape[0]
    cmp_end = lax.broadcasted_iota(jnp.int32, (1, nc), 1) * NSA_CMP_STRIDE + (NSA_CMP_LEN - 1)
    keep_c = jnp.where(pos_q >= cmp_end, 1.0, 0.0)
    p_c = _masked_softmax2(_dot_nt(q4, kc).reshape(H, tq, nc), keep_c)
    o_cmp = _dot(p_c.reshape(H * tq, nc).astype(BF16), vc).reshape(H, tq, HEAD_DIM)

    nw = NSA_WINDOW + tq
    start = pl.multiple_of(jnp.maximum(lo - NSA_WINDOW, 0), tq)
    kw = kw_ref[0, pl.ds(start, nw), :]
    vw = vw_ref[0, pl.ds(start, nw), :]
    kpos = start + lax.broadcasted_iota(jnp.int32, (1, nw), 1)
    keep_w = jnp.where((kpos <= pos_q) & (kpos > pos_q - NSA_WINDOW), 1.0, 0.0)
    s_w = _dot_nt(q4, kw).reshape(H, tq, nw) + ((1.0 - keep_w) * NEG_INF)[None]
    p_w = jnp.exp2((s_w - jnp.max(s_w, axis=-1, keepdims=True)).astype(BF16)) * keep_w.astype(BF16)[None]
    o_swa = _dot(p_w.reshape(H * tq, nw), vw).reshape(H, tq, 2 * HEAD_DIM)
    o_swa = o_swa[..., :HEAD_DIM] / jnp.maximum(o_swa[..., HEAD_DIM:HEAD_DIM + 1], 1e-30)

    p_sum = p_c[0] + p_c[1] + p_c[2] + p_c[3]
    hi, mid, lo3 = _split3(p_sum)
    ovl = ovl_ref[...]
    imp = _dot(hi, ovl) + _dot(mid, ovl) + _dot(lo3, ovl)
    cur = pos_q // NSA_SEL_LEN
    forced = (lane == 0) | (lane == cur) | (lane == cur - 1)
    score = jnp.where(forced, NSA_FORCE_SCORE, imp)
    score = jnp.where(lane <= cur, score, NEG_INF)
    n_blk = exp_ref.shape[1] // NSA_SEL_LEN
    score_t = score.T[:n_blk]
    blk = lax.broadcasted_iota(jnp.int32, (n_blk, 1), 0)
    cnt = jnp.zeros(score_t.shape, F32)
    for kk in range(n_blk):
        row = score_t[kk:kk + 1, :]
        beats = (row > score_t) | ((row == score_t) & (blk > kk))
        cnt = cnt + jnp.where(beats, 1.0, 0.0)
    sel_t = jnp.where(cnt < min(NSA_SEL_BLOCKS, n_blk), 1.0, 0.0)
    sel = jnp.concatenate([sel_t, jnp.zeros((LANES - n_blk, tq), F32)], axis=0).T.astype(BF16)

    keep_tri, _ = _tri_masks(tq)

    def slc_variant(i):
        lo_k, hi_k = i * tq, (i + 1) * tq
        keep = _dot(sel, exp_ref[:, :hi_k])
        m = jnp.full((H, tq, 1), NEG_INF, F32)
        acc = jnp.zeros((H, tq, 2 * HEAD_DIM), F32)
        for c in range(i + 1):
            ks = slice(c * tq, (c + 1) * tq)
            keep_c = keep[:, ks] * keep_tri if c == i else keep[:, ks]
            s = _dot_nt(q4, ksl_ref[0, ks, :]).reshape(H, tq, tq) + ((1.0 - keep_c) * NEG_INF)[None]
            m_new = jnp.maximum(m, jnp.max(s, axis=-1, keepdims=True))
            p = jnp.exp2((s - m_new).astype(BF16)) * keep_c.astype(BF16)[None]
            pv = _dot(p.reshape(H * tq, tq), vsl_ref[0, ks, :]).reshape(H, tq, 2 * HEAD_DIM)
            acc = jnp.exp2(m - m_new) * acc + pv
            m = m_new
        slc_ref[...] = acc[..., :HEAD_DIM] / jnp.maximum(acc[..., HEAD_DIM:HEAD_DIM + 1], 1e-30)

    for i in range(ksl_ref.shape[1] // tq):
        pl.when(qi == i)(functools.partial(slc_variant, i))
    o_slc = slc_ref[...]

    g = gate_ref[0]
    o_ref[0] = jnp.concatenate(
        [g[:, hh:hh + 1] * o_cmp[hh] + g[:, H + hh:H + hh + 1] * o_slc[hh] + g[:, 2 * H + hh:2 * H + hh + 1] * o_swa[hh]
         for hh in range(H)], axis=-1)


def _nsa_attn(q, cmp, ksl, vsl, kw, vw, gates, ovl, expand, tq=256):
    B, H, S, d = q.shape
    R = cmp.shape[2]

    def full(w):
        return pl.BlockSpec((1, S, w), lambda b, i: (b, 0, 0))

    return pl.pallas_call(
        functools.partial(_nsa_attn_kernel, tq=tq),
        grid=(B, S // tq),
        in_specs=[
            pl.BlockSpec((1, H, tq, d), lambda b, i: (b, 0, i, 0)),
            pl.BlockSpec((2, 1, R, d), lambda b, i: (0, b, 0, 0)),
            full(d), full(2 * d), full(d), full(2 * d),
            pl.BlockSpec((1, tq, LANES), lambda b, i: (b, i, 0)),
            pl.BlockSpec((R, LANES), lambda b, i: (0, 0)),
            pl.BlockSpec((LANES, S), lambda b, i: (0, 0)),
        ],
        out_specs=pl.BlockSpec((1, tq, H * d), lambda b, i: (b, i, 0)),
        out_shape=jax.ShapeDtypeStruct((B, S, H * d), F32),
        scratch_shapes=[pltpu.VMEM((H, tq, d), F32)],
        compiler_params=_params(("parallel", "arbitrary")),
        name="nsa_attn",
    )(q, cmp, ksl, vsl, kw, vw, gates, ovl, expand)


def _s5_kernel(u_ref, bre_ref, bim_ref, kr_ref, ki_ref, ar_ref, ai_ref, cre_ref, cim_ref, d_ref, y_ref,
               xr_ref, xi_ref, sr_ref, si_ref, us_ref, *, tc, nb, lane_blk):
    @pl.when(pl.program_id(0) == 0)
    def _():
        sr_ref[...] = jnp.zeros(sr_ref.shape, F32)
        si_ref[...] = jnp.zeros(si_ref.shape, F32)

    W = S5_WIDTH
    ncol = W // LANES
    for b in range(nb):
        for c in range(ncol):
            us_ref[c, pl.ds(b, tc, stride=nb), :] = u_ref[:, b * W + c * LANES:b * W + (c + 1) * LANES]
    u = jnp.concatenate([us_ref[c] for c in range(ncol)], axis=-1)
    ub = u.astype(BF16)
    kr, ki = kr_ref[...], ki_ref[...]
    bre, bim = bre_ref[...], bim_ref[...]
    xr_ref[...] = _dot(ub, (bre * kr - bim * ki).astype(BF16))
    xi_ref[...] = _dot(ub, (bre * ki + bim * kr).astype(BF16))

    for c in range(S5_STATES // lane_blk):
        sl = slice(c * lane_blk, (c + 1) * lane_blk)
        ar = jnp.broadcast_to(ar_ref[:, sl], (nb, lane_blk))
        ai = jnp.broadcast_to(ai_ref[:, sl], (nb, lane_blk))

        def step(t, carry):
            sr, si = carry
            row = pl.ds(pl.multiple_of(t * nb, nb), nb)
            nr = ar * sr - ai * si + xr_ref[row, sl]
            ni = ar * si + ai * sr + xi_ref[row, sl]
            xr_ref[row, sl] = nr
            xi_ref[row, sl] = ni
            return nr, ni

        sr, si = lax.fori_loop(0, tc, step, (sr_ref[:, sl], si_ref[:, sl]), unroll=4)
        sr_ref[:, sl] = sr
        si_ref[:, sl] = si

    y = (_dot(xr_ref[...].astype(BF16), cre_ref[...]) - _dot(xi_ref[...].astype(BF16), cim_ref[...])
         + d_ref[...] * u)
    for c in range(ncol):
        us_ref[c] = y[:, c * LANES:(c + 1) * LANES]
    for b in range(nb):
        for c in range(ncol):
            y_ref[:, b * W + c * LANES:b * W + (c + 1) * LANES] = us_ref[c, pl.ds(b, tc, stride=nb), :]


def _s5_scan(u_t, bre, bim, kr, ki, ar, ai, cre, cim, d, layer, tc=128, lane_blk=512):
    S = u_t.shape[0]
    W = S5_WIDTH
    B = u_t.shape[1] // W
    const = lambda shape: pl.BlockSpec((None,) + shape, lambda i: (layer, 0, 0))
    return pl.pallas_call(
        functools.partial(_s5_kernel, tc=tc, nb=B, lane_blk=lane_blk),
        grid=(S // tc,),
        in_specs=[
            pl.BlockSpec((tc, B * W), lambda i: (i, 0)),
            const((W, S5_STATES)), const((W, S5_STATES)),
            const((1, S5_STATES)), const((1, S5_STATES)), const((1, S5_STATES)), const((1, S5_STATES)),
            const((S5_STATES, W)), const((S5_STATES, W)), const((1, W)),
        ],
        out_specs=pl.BlockSpec((tc, B * W), lambda i: (i, 0)),
        out_shape=jax.ShapeDtypeStruct((S, B * W), F32),
        scratch_shapes=[
            pltpu.VMEM((tc * B, S5_STATES), F32), pltpu.VMEM((tc * B, S5_STATES), F32),
            pltpu.VMEM((B, S5_STATES), F32), pltpu.VMEM((B, S5_STATES), F32),
            pltpu.VMEM((W // LANES, tc * B, LANES), F32),
        ],
        compiler_params=_params(("arbitrary",)),
        name="s5_scan",
    )(u_t, bre, bim, kr, ki, ar, ai, cre, cim, d)


def _mix_out_kernel(h_ref, om_ref, on_ref, of_ref, ys_ref, gw_ref, gb_ref, gg_ref, wo_ref, pg_ref, o_ref):
    gg = gg_ref[0]
    y = jax.nn.gelu(ys_ref[...])
    y = y * jax.nn.sigmoid(_dot(y.astype(BF16), gw_ref[0]) + gb_ref[...])
    groups = (om_ref[0], on_ref[0], of_ref[0], y)
    cat = jnp.concatenate([_rms(p, gg[i:i + 1, :]).astype(BF16) for i, p in enumerate(groups)], axis=-1)
    o_ref[0] = h_ref[0] + _rms(_dot(cat, wo_ref[0]), pg_ref[...])


def _mix_out(h, o_mla, o_nsa, o_fox, y_s5, glu_w, glu_b, group_g, w_out, post_g, layer, tm=1024):
    B, S, D = h.shape

    def group():
        return pl.BlockSpec((1, tm, GROUP_WIDTH), lambda b, t: (b, t, 0))

    return pl.pallas_call(
        _mix_out_kernel,
        grid=(B, S // tm),
        in_specs=[
            pl.BlockSpec((1, tm, D), lambda b, t: (b, t, 0)),
            group(), group(), group(),
            pl.BlockSpec((tm, S5_WIDTH), lambda b, t: (t, b)),
            pl.BlockSpec((1, S5_WIDTH, S5_WIDTH), lambda b, t: (layer, 0, 0)),
            pl.BlockSpec((None, 1, S5_WIDTH), lambda b, t: (layer, 0, 0)),
            pl.BlockSpec((1, 4, GROUP_WIDTH), lambda b, t: (layer, 0, 0)),
            pl.BlockSpec((1, D, D), lambda b, t: (layer, 0, 0)),
            pl.BlockSpec((None, 1, D), lambda b, t: (layer, 0, 0)),
        ],
        out_specs=pl.BlockSpec((1, tm, D), lambda b, t: (b, t, 0)),
        out_shape=jax.ShapeDtypeStruct((B, S, D), F32),
        compiler_params=_params(("parallel", "parallel")),
        name="mix_out",
    )(h, o_mla, o_nsa, o_fox, y_s5, glu_w, glu_b, group_g, w_out, post_g)


def _xa_kv_kernel(mem_ref, g_ref, wkv_ref, kv_ref):
    m = _rms(mem_ref[0], g_ref[...]).astype(BF16)
    kv_ref[0] = _dot(m, wkv_ref[0]).astype(BF16)


def _xa_kv(mem, g, wkv, layer):
    B, M, D = mem.shape
    return pl.pallas_call(
        _xa_kv_kernel,
        grid=(B,),
        in_specs=[
            pl.BlockSpec((1, M, D), lambda b: (b, 0, 0)),
            pl.BlockSpec((None, 1, D), lambda b: (layer, 0, 0)),
            pl.BlockSpec((1, D, 2 * D), lambda b: (layer, 0, 0)),
        ],
        out_specs=pl.BlockSpec((1, M, 2 * D), lambda b: (b, 0, 0)),
        out_shape=jax.ShapeDtypeStruct((B, M, 2 * D), BF16),
        compiler_params=_params(("parallel",)),
        name="xa_kv",
    )(mem, g, wkv)


def _xa_kernel(h_ref, kv_ref, pre_g_ref, wq_ref, wo_ref, post_g_ref, o_ref):
    h = h_ref[0]
    u = _rms(h, pre_g_ref[...]).astype(BF16)
    q = _dot(u, wq_ref[0]) * (XA_HEAD_DIM ** -0.5)
    acc = jnp.zeros(h.shape, F32)
    for hh in range(XA_HEADS):
        sl = slice(hh * XA_HEAD_DIM, (hh + 1) * XA_HEAD_DIM)
        k = kv_ref[0, :, sl]
        v = kv_ref[0, :, D_MODEL + hh * XA_HEAD_DIM:D_MODEL + (hh + 1) * XA_HEAD_DIM]
        s = _dot_nt(q[:, sl].astype(BF16), k)
        m = jnp.max(s, axis=-1, keepdims=True)
        e = jnp.exp(s - m)
        p = e / jnp.sum(e, axis=-1, keepdims=True)
        o = _dot(p.astype(BF16), v)
        acc = acc + _dot(o.astype(BF16), wo_ref[0, sl, :])
    o_ref[0] = h + _rms(acc, post_g_ref[...])


def _xa(h, kv, pre_g, wq, wo, post_g, layer, tm=1024):
    B, S, D = h.shape
    M = kv.shape[1]
    return pl.pallas_call(
        _xa_kernel,
        grid=(B, S // tm),
        in_specs=[
            pl.BlockSpec((1, tm, D), lambda b, t: (b, t, 0)),
            pl.BlockSpec((1, M, 2 * D), lambda b, t: (b, 0, 0)),
            pl.BlockSpec((None, 1, D), lambda b, t: (layer, 0, 0)),
            pl.BlockSpec((1, D, D), lambda b, t: (layer, 0, 0)),
            pl.BlockSpec((1, D, D), lambda b, t: (layer, 0, 0)),
            pl.BlockSpec((None, 1, D), lambda b, t: (layer, 0, 0)),
        ],
        out_specs=pl.BlockSpec((1, tm, D), lambda b, t: (b, t, 0)),
        out_shape=jax.ShapeDtypeStruct((B, S, D), F32),
        compiler_params=_params(("parallel", "parallel")),
        name="cross_attn",
    )(h, kv, pre_g, wq, wo, post_g)


def _rope_tables(n, dim):
    inv = 1.0 / (ROPE_THETA ** (jnp.arange(0, dim, 2, dtype=F32) / dim))
    ang = jnp.arange(n, dtype=F32)[:, None] * inv[None, :]
    return jnp.cos(ang), jnp.sin(ang)


def _pack_w_in(w):
    L, D, _ = w.shape
    z = lambda n: jnp.zeros((L, D, n), w.dtype)
    o = 0
    cuts = {}
    for name, n in (("cq", MLA_Q_LORA), ("ckv", MLA_KV_LORA), ("kpe", MLA_ROPE_DIM), ("nq", 256), ("kc", 64),
                    ("vc", 64), ("ksl", 64), ("vsl", 64), ("kw", 64), ("vw", 64), ("gn", N_GATES),
                    ("fq", 256), ("fk", 256), ("fv", 256), ("ff", FOX_HEADS), ("s5", S5_WIDTH)):
        cuts[name] = w[:, :, o:o + n]
        o += n
    c = cuts
    return jnp.concatenate([
        c["cq"], z(64), c["ckv"], z(MLA_NOPE_DIM), c["kpe"], z(32), c["nq"], c["kc"], c["vc"], c["ksl"], c["vsl"],
        c["kw"], c["vw"], c["gn"], c["ff"], z(LANES - N_GATES - FOX_HEADS), c["fq"], c["fk"], c["fv"], c["s5"],
    ], axis=-1)


def kernel(x, mem, ffn1_pre_g, ffn1_w_gate, ffn1_w_up, ffn1_w_down, ffn1_post_g, mix_pre_g, mix_w_in, mla_q_norm_g, mla_w_uq, mla_kv_norm_g, mla_w_ukv, nsa_cmp_pos, nsa_phi_w1, nsa_phi_b1, nsa_phi_w2, nsa_gate_b, fox_f_b, s5_lambda_re, s5_lambda_im, s5_log_dt, s5_b_re, s5_b_im, s5_c_re, s5_c_im, s5_d, s5_glu_w, s5_glu_b, mix_group_g, mix_w_out, mix_post_g, xa_pre_g, xa_mem_g, xa_w_q, xa_w_kv, xa_w_o, xa_post_g, ffn2_pre_g, ffn2_w_gate, ffn2_w_up, ffn2_w_down, ffn2_post_g):
    B, S, D = x.shape
    L = ffn1_pre_g.shape[0]
    T = B * S
    G, P, C = S5_GROUPS, S5_STATE, S5_GROUP
    bf = lambda a: a.astype(BF16)

    w_in = bf(_pack_w_in(mix_w_in))
    qg = jnp.pad(mla_q_norm_g, ((0, 0), (0, 256 - MLA_Q_LORA)))
    wuq = mla_w_uq.reshape(L, MLA_Q_LORA, MLA_HEADS, MLA_NOPE_DIM + MLA_ROPE_DIM)
    wuq = jnp.pad(wuq, ((0, 0), (0, 256 - MLA_Q_LORA), (0, 0), (0, MLA_PAD_DIM - MLA_NOPE_DIM - MLA_ROPE_DIM)))
    wuq = bf(wuq.reshape(L, 256, MLA_HEADS * MLA_PAD_DIM))
    wukv = mla_w_ukv.reshape(L, MLA_KV_LORA, MLA_HEADS, MLA_NOPE_DIM + MLA_V_DIM)
    wk = jnp.pad(wukv[..., :MLA_NOPE_DIM], ((0, 0), (0, 0), (0, 0), (0, MLA_PAD_DIM - MLA_NOPE_DIM)))
    wukv = bf(jnp.concatenate([wk.reshape(L, MLA_KV_LORA, -1), wukv[..., MLA_NOPE_DIM:].reshape(L, MLA_KV_LORA, -1)], axis=-1))
    misc_bias = jnp.concatenate([nsa_gate_b, fox_f_b, jnp.zeros((L, LANES - N_GATES - FOX_HEADS), F32)], axis=-1)

    c32, s32 = _rope_tables(S, HEAD_DIM)
    cos64 = jnp.tile(jnp.concatenate([c32, c32], axis=-1), (1, 2))
    sin64 = jnp.tile(jnp.concatenate([-s32, s32], axis=-1), (1, 2))
    c16, s16 = _rope_tables(S, MLA_ROPE_DIM)
    one, zero = jnp.ones((S, 1), F32), jnp.zeros((S, 1), F32)
    cosm = jnp.concatenate([jnp.tile(one, (1, 64)), c16, c16, jnp.tile(one, (1, 32))], axis=-1)
    sinm = jnp.concatenate([jnp.tile(zero, (1, 64)), -s16, s16, jnp.tile(zero, (1, 32))], axis=-1)

    cmp_pos = jnp.concatenate([nsa_cmp_pos[:, 0], nsa_cmp_pos[:, 1]], axis=-1)
    w1 = nsa_phi_w1.reshape(L, 2, NSA_CMP_LEN, HEAD_DIM, NSA_CMP_HIDDEN)
    w1 = jnp.stack([jnp.pad(w1[:, 0], ((0, 0), (0, 0), (0, HEAD_DIM), (0, 0))),
                    jnp.pad(w1[:, 1], ((0, 0), (0, 0), (HEAD_DIM, 0), (0, 0)))], axis=1)
    phi_w1 = bf(w1.reshape(L, 2, NSA_CMP_LEN * LANES, NSA_CMP_HIDDEN))
    phi_w2 = bf(nsa_phi_w2)
    phi_b1 = nsa_phi_b1.reshape(L, 2, 1, NSA_CMP_HIDDEN)

    n_cmp_pad = S // NSA_CMP_STRIDE
    n_blk = S // NSA_SEL_LEN
    cs = np.arange(n_cmp_pad) * NSA_CMP_STRIDE
    ss = np.arange(LANES) * NSA_SEL_LEN
    ovl_np = ((cs[:, None] < ss[None, :] + NSA_SEL_LEN) & (cs[:, None] + NSA_CMP_LEN > ss[None, :])
              & (np.arange(LANES)[None, :] < n_blk) & (np.arange(n_cmp_pad)[:, None] < n_cmp_pad - 1))
    ovl = jnp.asarray(ovl_np.astype(np.float32), dtype=BF16)
    tq_nsa = 256
    key_blk = np.arange(S) // NSA_SEL_LEN
    expand = jnp.asarray((key_blk[None, :] == np.arange(LANES)[:, None]).astype(np.float32), dtype=BF16)

    dt = jnp.exp(s5_log_dt)[:, :, None]
    lr, li = s5_lambda_re, s5_lambda_im
    mag = jnp.exp(lr * dt)
    a_r, a_i = mag * jnp.cos(li * dt), mag * jnp.sin(li * dt)
    den = lr * lr + li * li
    k_r = ((a_r - 1.0) * lr + a_i * li) / den
    k_i = (a_i * lr - (a_r - 1.0) * li) / den
    flat = lambda a: a.reshape(L, 1, G * P)
    a_r, a_i, k_r, k_i = flat(a_r), flat(a_i), flat(k_r), flat(k_i)
    eye = jnp.eye(G, dtype=F32)
    blk_b = lambda b: jnp.einsum('lgpc,gh->lgchp', b, eye).reshape(L, G * C, G * P)
    blk_c = lambda c: jnp.einsum('lgcp,gh->lgphc', c, eye).reshape(L, G * P, G * C)
    b_re, b_im = blk_b(s5_b_re), blk_b(s5_b_im)
    c_re, c_im = bf(blk_c(s5_c_re)), bf(blk_c(s5_c_im))
    d_skip = s5_d.reshape(L, 1, S5_WIDTH)

    ffn_w = [(bf(ffn1_w_gate), bf(ffn1_w_up), bf(ffn1_w_down)), (bf(ffn2_w_gate), bf(ffn2_w_up), bf(ffn2_w_down))]
    glu_w, w_out = bf(s5_glu_w), bf(mix_w_out)
    wq_x, wkv_x, wo_x = bf(xa_w_q), bf(xa_w_kv), bf(xa_w_o)

    v3 = lambda a: a.reshape(L, 1, a.shape[-1])
    ffn1_pre_g, ffn1_post_g, ffn2_pre_g, ffn2_post_g = v3(ffn1_pre_g), v3(ffn1_post_g), v3(ffn2_pre_g), v3(ffn2_post_g)
    mix_pre_g, mix_post_g, qg, kvg, misc_bias = v3(mix_pre_g), v3(mix_post_g), v3(qg), v3(mla_kv_norm_g), v3(misc_bias)
    s5_glu_b, xa_mem_g, xa_pre_g, xa_post_g = v3(s5_glu_b), v3(xa_mem_g), v3(xa_pre_g), v3(xa_post_g)

    h = x
    for l in range(L):
        h = _ffn(h.reshape(T, D), ffn1_pre_g, *ffn_w[0], ffn1_post_g, l).reshape(B, S, D)

        (mq, mk, mv, nq, ncmp, nksl, nvsl, nkw, nvw, misc, fq, fk, fv, s5u) = _proj(
            h, mix_pre_g, w_in, qg, wuq, kvg, wukv, misc_bias, cos64, sin64, cosm, sinm, l)

        o_mla = _causal_attn(mq, mk, mv, "mla_attn")

        o_fox = _causal_attn(fq, fk, fv, "fox_attn")

        cmp = _nsa_compress(ncmp, cmp_pos, phi_w1, phi_b1, phi_w2, l)
        o_nsa = _nsa_attn(nq, cmp, nksl, nvsl, nkw, nvw, misc, ovl, expand, tq=tq_nsa)

        y_s5 = _s5_scan(s5u, b_re, b_im, k_r, k_i, a_r, a_i, c_re, c_im, d_skip, l)

        h = _mix_out(h, o_mla, o_nsa, o_fox, y_s5, glu_w, s5_glu_b, mix_group_g, w_out, mix_post_g, l)

        kv = _xa_kv(mem, xa_mem_g, wkv_x, l)
        h = _xa(h, kv, xa_pre_g, wq_x, wo_x, xa_post_g, l)

        h = _ffn(h.reshape(T, D), ffn2_pre_g, *ffn_w[1], ffn2_post_g, l).reshape(B, S, D)
    return h
```
